```python
import math
import jax, jax.numpy as jnp
from jax import lax
import numpy as np

D_MODEL = 1024
BATCH = 8
SEQ = 2048
DEPTH = 2
DEC_BATCH = 128
DEC_SEQ = 8
PAST_LEN = 16384
PAGE_SIZE = 128

A_HEADS = 8
A_HEAD = 64
A_WIDTH = A_HEADS * A_HEAD
LORA_W = 64
LORA_A = 64
LORA_G = 160
SHIFT_W = 3 * A_WIDTH + LORA_W + LORA_A + LORA_G
A_SPLITS = [A_WIDTH, 2 * A_WIDTH, 3 * A_WIDTH, 3 * A_WIDTH + LORA_W, 3 * A_WIDTH + LORA_W + LORA_A]
GN_EPS_A = 64e-5
B_HEADS = 8
B_DK = 64
B_DV = 128
B_QK = B_HEADS * B_DK
B_V = B_HEADS * B_DV
RET_W = 2 * B_QK + 2 * B_V
RET_CHUNK = 128
ROPE_BASE = 10000.0
GN_EPS_B = 1e-5
IN_W = SHIFT_W + RET_W + 2 * D_MODEL
D_FF = -(-8 * D_MODEL // (3 * 256)) * 256
RMS_EPS = 1e-6

kernel_name = 'hybrid_rwkv7_retention_decoder'


def _rmsnorm(x, g):
    xf = x.astype(jnp.float32)
    y = xf * lax.rsqrt(jnp.mean(xf * xf, -1, keepdims=True) + RMS_EPS)
    return (y * g.astype(jnp.float32)).astype(x.dtype)


def _head_norm(y, gain, bias, eps):
    b, l, h, d = y.shape
    mu = jnp.mean(y, -1, keepdims=True)
    var = jnp.mean(jnp.square(y - mu), -1, keepdims=True)
    yn = ((y - mu) * lax.rsqrt(var + eps)).reshape(b, l, h * d)
    return yn * gain.astype(jnp.float32) + bias.astype(jnp.float32)


def _rotary(x, pos):
    half = x.shape[-1] // 2
    inv = ROPE_BASE ** (-jnp.arange(half, dtype=jnp.float32) / half)
    ang = pos.astype(jnp.float32)[:, None] * inv[None, :]
    cos = jnp.cos(ang)[None, :, None, :]
    sin = jnp.sin(ang)[None, :, None, :]
    xf = x.astype(jnp.float32)
    x1, x2 = xf[..., :half], xf[..., half:]
    return jnp.concatenate([x1 * cos - x2 * sin, x1 * sin + x2 * cos], axis=-1)


def _rwkv7_branch(za, s_wkv, s_shift, mu, w0, lw_up, a0, la_up, lg_up, k_k, k_a, r_k, gn_g, gn_b):
    f32 = jnp.float32
    bsz, L, _ = za.shape
    prev = jnp.concatenate([s_shift[:, None].astype(za.dtype), za[:, :-1]], axis=1)
    zs = za + mu * (prev - za)
    r, k, v, dw, da, dg = jnp.split(zs, A_SPLITS, axis=-1)
    heads = lambda t: t.reshape(bsz, L, A_HEADS, A_HEAD)
    w = -jax.nn.softplus(-(w0 + jnp.tanh(dw) @ lw_up).astype(f32)) - 0.5
    decay = heads(jnp.exp(-jnp.exp(w)))
    a = heads(jax.nn.sigmoid((a0 + da @ la_up).astype(f32)))
    g = (jax.nn.sigmoid(dg) @ lg_up).astype(f32)
    r32, k32, v32 = heads(r.astype(f32)), heads(k.astype(f32)), heads(v.astype(f32))
    kk = k32 * k_k.astype(f32).reshape(A_HEADS, A_HEAD)
    kk = kk * lax.rsqrt(jnp.maximum(jnp.sum(kk * kk, -1, keepdims=True), 1e-24))
    k32 = k32 * (1.0 + (a - 1.0) * k_a.astype(f32).reshape(A_HEADS, A_HEAD))
    b = kk * a

    def step(S, inp):
        r_t, w_t, k_t, v_t, kk_t, b_t = inp
        sk = jnp.einsum('bhvk,bhk->bhv', S, kk_t)
        S = S * w_t[:, :, None, :] - sk[..., None] * b_t[:, :, None, :] + v_t[..., None] * k_t[:, :, None, :]
        return S, jnp.einsum('bhvk,bhk->bhv', S, r_t)

    xs = tuple(jnp.swapaxes(t, 0, 1) for t in (r32, decay, k32, v32, kk, b))
    S, ys = lax.scan(step, s_wkv.astype(f32), xs)
    y = _head_norm(jnp.swapaxes(ys, 0, 1), gn_g, gn_b, GN_EPS_A)
    bonus = jnp.sum(r32 * k32 * r_k.astype(f32), -1, keepdims=True) * v32
    y = (y + bonus.reshape(bsz, L, A_WIDTH)) * g
    return y.astype(za.dtype), S.astype(s_wkv.dtype), za[:, -1]


def _chunkwise_retention(q, k, v, S0):
    f32 = jnp.float32
    bsz, L = q.shape[:2]
    C = math.gcd(L, RET_CHUNK)
    n = L // C
    log_g = jnp.log(1.0 - jnp.exp2(-5.0 - jnp.arange(B_HEADS, dtype=f32)))
    idx = jnp.arange(C, dtype=f32)
    rel = idx[:, None] - idx[None, :]
    causal = rel >= 0
    d_in = jnp.where(causal[None], jnp.exp(log_g[:, None, None] * jnp.where(causal, rel, 0.0)[None]), 0.0)
    d_q = jnp.exp(log_g[None, :] * (idx[:, None] + 1.0))
    d_k = jnp.exp(log_g[None, :] * (C - 1.0 - idx[:, None]))
    d_c = jnp.exp(log_g * C)
    split = lambda t: jnp.moveaxis(t.reshape(bsz, n, C, *t.shape[2:]), 1, 0)

    def step(S, inp):
        qc, kc, vc = inp
        scores = jnp.einsum('bihd,bjhd->bhij', qc, kc) * d_in[None]
        inner = jnp.einsum('bhij,bjhe->bihe', scores, vc)
        cross = jnp.einsum('bihd,bhde->bihe', qc, S) * d_q[None, :, :, None]
        S = S * d_c[None, :, None, None] + jnp.einsum('bjhd,bjhe->bhde', kc * d_k[None, :, :, None], vc)
        return S, inner + cross

    S, o = lax.scan(step, S0.astype(f32), (split(q), split(k), split(v.astype(f32))))
    o = jnp.moveaxis(o, 0, 1).reshape(bsz, L, B_HEADS, B_DV)
    return o, S


def _retention_branch(zb, pos, s_ret, gn_g, gn_b):
    bsz, L, _ = zb.shape
    q, k, v, gate = jnp.split(zb, [B_QK, 2 * B_QK, 2 * B_QK + B_V], axis=-1)
    q = _rotary(q.reshape(bsz, L, B_HEADS, B_DK), pos)
    k = _rotary(k.reshape(bsz, L, B_HEADS, B_DK), pos) * (B_DK ** -0.5)
    v = v.reshape(bsz, L, B_HEADS, B_DV)
    o, S = _chunkwise_retention(q, k, v, s_ret)
    y = _head_norm(o, gn_g, gn_b, GN_EPS_B) * jax.nn.silu(gate.astype(jnp.float32))
    return y.astype(zb.dtype), S.astype(s_ret.dtype)


def _layer(x, pos, s_wkv, s_ret, s_shift, p):
    h = _rmsnorm(x, p['norm_mix_pre'])
    z = h @ p['w_in']
    za = z[..., :SHIFT_W]
    zb = z[..., SHIFT_W:SHIFT_W + RET_W]
    ga, gb = jnp.split(z[..., SHIFT_W + RET_W:], 2, axis=-1)
    ya, s_wkv, s_shift = _rwkv7_branch(za, s_wkv, s_shift, p['mu_shift'], p['w0'], p['lora_w_up'], p['a0'],
                                       p['lora_a_up'], p['lora_g_up'], p['k_k'], p['k_a'], p['r_k'],
                                       p['gn_a_gain'], p['gn_a_bias'])
    yb, s_ret = _retention_branch(zb, pos, s_ret, p['gn_b_gain'], p['gn_b_bias'])
    m = jax.nn.sigmoid(ga) * (ya @ p['w_out_a']) + jax.nn.sigmoid(gb) * (yb @ p['w_out_b'])
    x = x + _rmsnorm(m @ p['w_o'], p['norm_mix_post'])
    h = _rmsnorm(x, p['norm_ffn_pre'])
    f = (jax.nn.silu(h @ p['w_ffn_gate']) * (h @ p['w_ffn_up'])) @ p['w_ffn_down']
    x = x + _rmsnorm(f, p['norm_ffn_post'])
    return x, s_wkv, s_ret, s_shift


def setup_inputs(seed: int = 0) -> dict:
    key = jax.random.key(seed)
    ks = jax.random.split(key, 32)
    f32 = jnp.float32
    nrm = lambda k, shape, s: jax.random.normal(k, shape, f32) * s
    gain = lambda k, shape: 1.0 + 0.05 * jax.random.normal(k, shape, f32)
    return {
        'x_prompt': nrm(ks[0], (BATCH, SEQ, D_MODEL), 1.0),
        'x_sample': nrm(ks[1], (DEC_BATCH, DEC_SEQ, D_MODEL), 1.0),
        'state_wkv': nrm(ks[2], (DEPTH, DEC_BATCH, A_HEADS, A_HEAD, A_HEAD), 0.1),
        'state_ret': nrm(ks[3], (DEPTH, DEC_BATCH, B_HEADS, B_DK, B_DV), 0.1),
        'state_shift': nrm(ks[4], (DEPTH, DEC_BATCH, SHIFT_W), 1.0),
        'norm_mix_pre': gain(ks[5], (DEPTH, D_MODEL)),
        'w_in': nrm(ks[6], (DEPTH, D_MODEL, IN_W), D_MODEL ** -0.5),
        'mu_shift': jax.random.uniform(ks[7], (DEPTH, SHIFT_W), f32),
        'w0': jax.random.uniform(ks[8], (DEPTH, A_WIDTH), f32, -6.0, 0.0),
        'lora_w_up': nrm(ks[9], (DEPTH, LORA_W, A_WIDTH), 0.5 * LORA_W ** -0.5),
        'a0': nrm(ks[10], (DEPTH, A_WIDTH), 0.3),
        'lora_a_up': nrm(ks[11], (DEPTH, LORA_A, A_WIDTH), 0.5 * LORA_A ** -0.5),
        'lora_g_up': nrm(ks[12], (DEPTH, LORA_G, A_WIDTH), LORA_G ** -0.5),
        'k_k': 0.85 + 0.02 * jax.random.normal(ks[13], (DEPTH, A_WIDTH), f32),
        'k_a': gain(ks[14], (DEPTH, A_WIDTH)),
        'r_k': nrm(ks[15], (DEPTH, A_HEADS, A_HEAD), 0.1),
        'gn_a_gain': gain(ks[16], (DEPTH, A_WIDTH)),
        'gn_a_bias': nrm(ks[17], (DEPTH, A_WIDTH), 0.02),
        'w_out_a': nrm(ks[18], (DEPTH, A_WIDTH, D_MODEL), A_WIDTH ** -0.5),
        'gn_b_gain': gain(ks[19], (DEPTH, B_V)),
        'gn_b_bias': nrm(ks[20], (DEPTH, B_V), 0.02),
        'w_out_b': nrm(ks[21], (DEPTH, B_V, D_MODEL), B_V ** -0.5),
        'w_o': nrm(ks[22], (DEPTH, D_MODEL, D_MODEL), D_MODEL ** -0.5),
        'norm_mix_post': gain(ks[23], (DEPTH, D_MODEL)),
        'norm_ffn_pre': gain(ks[24], (DEPTH, D_MODEL)),
        'w_ffn_gate': nrm(ks[25], (DEPTH, D_MODEL, D_FF), D_MODEL ** -0.5),
        'w_ffn_up': nrm(ks[26], (DEPTH, D_MODEL, D_FF), D_MODEL ** -0.5),
        'w_ffn_down': nrm(ks[27], (DEPTH, D_FF, D_MODEL), D_FF ** -0.5),
        'norm_ffn_post': gain(ks[28], (DEPTH, D_MODEL)),
    }


def reference(x_prompt, x_sample, state_wkv, state_ret, state_shift, norm_mix_pre, w_in, mu_shift, w0,
              lora_w_up, a0, lora_a_up, lora_g_up, k_k, k_a, r_k, gn_a_gain, gn_a_bias, w_out_a,
              gn_b_gain, gn_b_bias, w_out_b, w_o, norm_mix_post, norm_ffn_pre, w_ffn_gate, w_ffn_up,
              w_ffn_down, norm_ffn_post):
    bp, lp = x_prompt.shape[0], x_prompt.shape[1]
    pos_p = jnp.arange(lp)
    pos_s = PAST_LEN + jnp.arange(x_sample.shape[1])
    yp, ys = x_prompt, x_sample
    wkv_p, ret_p, shf_p, wkv_s, ret_s, shf_s = [], [], [], [], [], []
    for l in range(DEPTH):
        p = {'norm_mix_pre': norm_mix_pre[l], 'w_in': w_in[l], 'mu_shift': mu_shift[l], 'w0': w0[l],
             'lora_w_up': lora_w_up[l], 'a0': a0[l], 'lora_a_up': lora_a_up[l], 'lora_g_up': lora_g_up[l],
             'k_k': k_k[l], 'k_a': k_a[l], 'r_k': r_k[l], 'gn_a_gain': gn_a_gain[l], 'gn_a_bias': gn_a_bias[l],
             'w_out_a': w_out_a[l], 'gn_b_gain': gn_b_gain[l], 'gn_b_bias': gn_b_bias[l], 'w_out_b': w_out_b[l],
             'w_o': w_o[l], 'norm_mix_post': norm_mix_post[l], 'norm_ffn_pre': norm_ffn_pre[l],
             'w_ffn_gate': w_ffn_gate[l], 'w_ffn_up': w_ffn_up[l], 'w_ffn_down': w_ffn_down[l],
             'norm_ffn_post': norm_ffn_post[l]}
        z_wkv = jnp.zeros((bp, A_HEADS, A_HEAD, A_HEAD), state_wkv.dtype)
        z_ret = jnp.zeros((bp, B_HEADS, B_DK, B_DV), state_ret.dtype)
        z_shf = jnp.zeros((bp, SHIFT_W), state_shift.dtype)
        yp, a_, b_, c_ = _layer(yp, pos_p, z_wkv, z_ret, z_shf, p)
        wkv_p.append(a_); ret_p.append(b_); shf_p.append(c_)
        ys, a_, b_, c_ = _layer(ys, pos_s, state_wkv[l], state_ret[l], state_shift[l], p)
        wkv_s.append(a_); ret_s.append(b_); shf_s.append(c_)
    return (yp, ys, jnp.stack(wkv_p), jnp.stack(ret_p), jnp.stack(shf_p),
            jnp.stack(wkv_s), jnp.stack(ret_s), jnp.stack(shf_s))
```

```python
import functools

import jax
import jax.numpy as jnp
from jax import lax
from jax.experimental import pallas as pl
from jax.experimental.pallas import tpu as pltpu

F32 = jnp.float32
BF16 = jnp.bfloat16

LANES = 128
D_MODEL = 1024
PAST_LEN = 16384
A_HEADS = 8
A_HEAD = 64
A_WIDTH = A_HEADS * A_HEAD
A_PAIRS = A_WIDTH // LANES
LORA_W = 64
LORA_A = 64
LORA_G = 160
SHIFT_W = 3 * A_WIDTH + LORA_W + LORA_A + LORA_G
SHIFT_PAD = 1920
GN_EPS_A = 64e-5
B_HEADS = 8
B_DK = 64
B_DV = 128
B_QK = B_HEADS * B_DK
B_V = B_HEADS * B_DV
RET_W = 2 * B_QK + 2 * B_V
RET_CHUNK = 128
ROPE_BASE = 10000.0
GN_EPS_B = 1e-5
D_FF = 2816
RMS_EPS = 1e-6
WKV_CHUNK = 64

VMEM_LIMIT = 56 * 1024 * 1024


def _resident(shape):
    nd = len(shape)
    return pl.BlockSpec(shape, lambda *_: (0,) * nd, pipeline_mode=pl.Buffered(1))


def _dot(a, b):
    return jnp.dot(a, b, preferred_element_type=F32)


def _dot_nt(a, b):
    return lax.dot_general(a, b, (((1,), (1,)), ((), ())), preferred_element_type=F32)


def _dot_tn(a, b):
    return lax.dot_general(a, b, (((0,), (0,)), ((), ())), preferred_element_type=F32)


def _rms(x, g):
    return x * lax.rsqrt(jnp.mean(x * x, -1, keepdims=True) + RMS_EPS) * g


def _half_sums(x):
    rows, width = x.shape
    lo = lax.broadcasted_iota(jnp.int32, (rows, LANES), 1) < A_HEAD
    out = []
    for p in range(width // LANES):
        xp = x[:, p * LANES:(p + 1) * LANES]
        s_lo = jnp.sum(jnp.where(lo, xp, 0.0), -1, keepdims=True)
        s_hi = jnp.sum(jnp.where(lo, 0.0, xp), -1, keepdims=True)
        out.append(jnp.where(lo, s_lo, s_hi))
    return jnp.concatenate(out, -1)


def _in_proj_kernel(x_ref, g_ref, wa_ref, wb_ref, wg_ref, za_ref, zb_ref, zg_ref):
    hb = _rms(x_ref[...], g_ref[...]).astype(BF16)
    za_ref[...] = _dot(hb, wa_ref[...])
    zb_ref[...] = _dot(hb, wb_ref[...])
    zg_ref[...] = _dot(hb, wg_ref[...])


def _in_proj(x, g, wa, wb, wg, tm):
    t = x.shape[0]
    row = lambda n: pl.BlockSpec((tm, n), lambda i: (i, 0))
    return pl.pallas_call(
        _in_proj_kernel,
        grid=(t // tm,),
        in_specs=[row(D_MODEL), _resident(g.shape), _resident(wa.shape), _resident(wb.shape),
                  _resident(wg.shape)],
        out_specs=[row(SHIFT_PAD), row(RET_W), row(2 * D_MODEL)],
        out_shape=[jax.ShapeDtypeStruct((t, SHIFT_PAD), F32), jax.ShapeDtypeStruct((t, RET_W), F32),
                   jax.ShapeDtypeStruct((t, 2 * D_MODEL), F32)],
        compiler_params=pltpu.CompilerParams(dimension_semantics=("parallel",),
                                             vmem_limit_bytes=VMEM_LIMIT),
        name="in_proj",
    )(x, g, wa, wb, wg)


def _softplus(y):
    return jnp.maximum(y, 0.0) + jnp.log(1.0 + jnp.exp(-jnp.abs(y)))


def _prep_kernel(*refs, seq_len, seg, has_shift):
    if has_shift:
        za_ref, halo_ref, shift_ref = refs[:3]
        refs = refs[3:]
    else:
        za_ref, halo_ref = refs[:2]
        shift_ref = None
        refs = refs[2:]
    (mu_ref, wlora_ref, lg_ref, w0_ref, a0_ref, kk_ref, ka_ref, rk_ref,
     lkk_ref, lr_ref, rhk_ref, rhb_ref, uk_ref, ub_ref, v_ref, wc_ref, g_ref, bonus_ref) = refs

    za = za_ref[...]
    rows = za.shape[0]
    row = lax.broadcasted_iota(jnp.int32, (rows, 1), 0)
    prev = pltpu.roll(za, 1, axis=0)
    prev = jnp.where(row == 0, halo_ref[7:8, :], prev)
    if has_shift:
        prev = jnp.where(row % seq_len == 0, shift_ref[...], prev)
    else:
        starts = (pl.program_id(0) * rows) % seq_len == 0
        prev = jnp.where((row == 0) & starts, 0.0, prev)
    zs = za + mu_ref[...] * (prev - za)

    r = zs[:, 0:A_WIDTH]
    k = zs[:, A_WIDTH:2 * A_WIDTH]
    v = zs[:, 2 * A_WIDTH:3 * A_WIDTH]
    dwa = zs[:, 3 * A_WIDTH:3 * A_WIDTH + LANES]
    dg = zs[:, 3 * A_WIDTH + LANES:SHIFT_PAD]
    lane = lax.broadcasted_iota(jnp.int32, (rows, LANES), 1)
    act = jnp.where(lane < LORA_W, jnp.tanh(dwa), dwa).astype(BF16)
    lora = _dot(act, wlora_ref[...])
    w = -_softplus(-(w0_ref[...] + lora[:, :A_WIDTH])) - 0.5
    lw = -jnp.exp(w)
    a = jax.nn.sigmoid(a0_ref[...] + lora[:, A_WIDTH:])
    g_ref[...] = _dot(jax.nn.sigmoid(dg).astype(BF16), lg_ref[...])

    kk = k * kk_ref[...]
    kk = kk * lax.rsqrt(jnp.maximum(_half_sums(kk * kk), 1e-24))
    k2 = k * (1.0 + (a - 1.0) * ka_ref[...])
    b = kk * a
    bonus_ref[...] = _half_sums(r * k2 * rk_ref[...]) * v
    v_ref[...] = v.astype(BF16)

    ti = lax.broadcasted_iota(jnp.int32, (LANES, LANES), 0)
    tj = lax.broadcasted_iota(jnp.int32, (LANES, LANES), 1)
    same = (ti // seg) == (tj // seg)
    tri = jnp.concatenate([jnp.where(same & (tj <= ti), 1.0, 0.0),
                           jnp.where(same, 1.0, 0.0)], 0).astype(BF16)
    for g0 in range(0, rows, LANES):
        sl = slice(g0, g0 + LANES)
        x = lw[sl]
        hi = x.astype(BF16)
        r1 = x - hi.astype(F32)
        mid = r1.astype(BF16)
        lo = (r1 - mid.astype(F32)).astype(BF16)
        ct = _dot(tri, hi) + _dot(tri, mid) + _dot(tri, lo)
        cum, tot = ct[:LANES], ct[LANES:]
        inv = jnp.exp(-cum)
        tail = jnp.exp(tot - cum)
        lkk_ref[sl, :] = (kk[sl] * jnp.exp(cum - x)).astype(BF16)
        lr_ref[sl, :] = (r[sl] * jnp.exp(cum)).astype(BF16)
        rhk_ref[sl, :] = (k2[sl] * inv).astype(BF16)
        rhb_ref[sl, :] = (b[sl] * inv).astype(BF16)
        uk_ref[sl, :] = (k2[sl] * tail).astype(BF16)
        ub_ref[sl, :] = (-(b[sl] * tail)).astype(BF16)
        for s in range(LANES // seg):
            wc_ref[g0 // seg + s] = jnp.exp(tot[s * seg:s * seg + 1, :])


def _prep(za, shift_rows, pp, seq_len, seg, rows):
    t = za.shape[0]
    has_shift = shift_rows is not None
    blk = lambda n: pl.BlockSpec((rows, n), lambda i: (i, 0))
    halo = pl.BlockSpec((8, SHIFT_PAD), lambda i: (jnp.maximum(i * (rows // 8) - 1, 0), 0))
    params = [pp['mu'], pp['wlora'], pp['lg'], pp['w0'], pp['a0'], pp['k_k'], pp['k_a'], pp['r_k']]
    ins = [za, za] + ([shift_rows] if has_shift else []) + params
    in_specs = [blk(SHIFT_PAD), halo] + ([blk(SHIFT_PAD)] if has_shift else []) + \
        [_resident(p.shape) for p in params]
    bf = jax.ShapeDtypeStruct((t, A_WIDTH), BF16)
    f32 = jax.ShapeDtypeStruct((t, A_WIDTH), F32)
    out_shape = [bf] * 7 + [jax.ShapeDtypeStruct((t // seg, 1, A_WIDTH), F32), f32, f32]
    out_specs = [blk(A_WIDTH)] * 7 + [pl.BlockSpec((rows // seg, 1, A_WIDTH), lambda i: (i, 0, 0)),
                                      blk(A_WIDTH), blk(A_WIDTH)]
    return pl.pallas_call(
        functools.partial(_prep_kernel, seq_len=seq_len, seg=seg, has_shift=has_shift),
        grid=(t // rows,),
        in_specs=in_specs, out_specs=out_specs, out_shape=out_shape,
        compiler_params=pltpu.CompilerParams(dimension_semantics=("parallel",),
                                             vmem_limit_bytes=VMEM_LIMIT),
        name="rwkv_prep",
    )(*ins)


def _scan_kernel(lkk_ref, lr_ref, rhk_ref, rhb_ref, uk_ref, ub_ref, v_ref, wc_ref, s0_ref,
                 g_ref, bonus_ref, gng_ref, gnb_ref, ya_ref, st_ref, *, nck, nseg, carry):
    c64 = WKV_CHUNK
    seg = c64 // nseg
    if carry:
        @pl.when(pl.program_id(1) == 0)
        def _():
            st_ref[...] = s0_ref[...]
    else:
        st_ref[...] = s0_ref[...]

    ti = lax.broadcasted_iota(jnp.int32, (LANES, LANES), 0)
    tj = lax.broadcasted_iota(jnp.int32, (LANES, LANES), 1)
    same_head = (ti // c64) == (tj // c64)
    same_seg = (ti // seg) == (tj // seg)
    strict = same_seg & (tj < ti)
    incl = same_seg & (tj <= ti)
    eye = jnp.where(ti == tj, 1.0, 0.0)
    lane_row = lax.broadcasted_iota(jnp.int32, (1, LANES), 1)
    m_lo = jnp.where(lane_row < A_HEAD, 1.0, 0.0).astype(BF16)
    m_hi = jnp.where(lane_row < A_HEAD, 0.0, 1.0).astype(BF16)
    lo_half = lax.broadcasted_iota(jnp.int32, (c64, LANES), 1) < A_HEAD
    stack_heads = lambda x: jnp.concatenate([x * m_lo, x * m_hi], 0)
    twice = lambda x: jnp.concatenate([x, x], 0)
    unstack = lambda x: jnp.where(lo_half, x[:c64], x[c64:])

    for c in range(nck):
        rs = slice(c * c64, (c + 1) * c64)
        y_pairs = []
        for p in range(A_PAIRS):
            cs = slice(p * LANES, (p + 1) * LANES)
            lk, lr = lkk_ref[rs, cs], lr_ref[rs, cs]
            rk2, rb2 = twice(rhk_ref[rs, cs]), twice(rhb_ref[rs, cs])
            v = v_ref[rs, cs]
            lkm, lrm = stack_heads(lk), stack_heads(lr)
            a_kk = jnp.where(strict, _dot_nt(lkm, rk2), 0.0)
            a_kb = jnp.where(strict, _dot_nt(lkm, rb2), 0.0)
            a_rk = jnp.where(incl, _dot_nt(lrm, rk2), 0.0)
            a_rb = jnp.where(incl, _dot_nt(lrm, rb2), 0.0)

            x = eye
            s = 1
            while s < seg:
                lvl = ((ti // s) % 2 == 1) & ((tj // s) == (ti // s) - 1)
                xb = x.astype(BF16)
                x = x - _dot(_dot(xb, jnp.where(lvl, a_kb, 0.0).astype(BF16)).astype(BF16), xb)
                s *= 2

            if nseg == 1:
                st = st_ref[0, p]
                g_k = _dot_nt(lk, st.astype(BF16))
                g_r = _dot_nt(lr, st.astype(BF16))
            else:
                lkf, lrf = lk.astype(F32), lr.astype(F32)
                gk, gr = [], []
                for s_ in range(nseg):
                    ss = slice(s_ * seg, (s_ + 1) * seg)
                    st = st_ref[c * nseg + s_, p]
                    gk.append(_dot_nt(lkf[ss], st))
                    gr.append(_dot_nt(lrf[ss], st))
                g_k, g_r = jnp.concatenate(gk, 0), jnp.concatenate(gr, 0)

            rhs = twice(g_k) + _dot(a_kk.astype(BF16), twice(v))
            sk = unstack(_dot(x.astype(BF16), rhs.astype(BF16)))
            skb = sk.astype(BF16)
            y2 = _dot(a_rk.astype(BF16), twice(v)) - _dot(a_rb.astype(BF16), twice(skb))
            y_pairs.append(g_r + unstack(y2))

            uk, ub = uk_ref[rs, cs], ub_ref[rs, cs]
            if nseg == 1:
                upd = _dot_tn(jnp.concatenate([v, skb], 0), jnp.concatenate([uk, ub], 0))
                st_ref[0, p] = st_ref[0, p] * wc_ref[c, :, cs] + jnp.where(same_head, upd, 0.0)
            else:
                vf, ukf, ubf = v.astype(F32), uk.astype(F32), ub.astype(F32)
                for s_ in range(nseg):
                    ss = slice(s_ * seg, (s_ + 1) * seg)
                    upd = _dot_tn(jnp.concatenate([vf[ss], sk[ss]], 0),
                                  jnp.concatenate([ukf[ss], ubf[ss]], 0))
                    i = c * nseg + s_
                    st_ref[i, p] = st_ref[i, p] * wc_ref[i, :, cs] + jnp.where(same_head, upd, 0.0)

        y = jnp.concatenate(y_pairs, -1)
        mu = _half_sums(y) * (1.0 / A_HEAD)
        d = y - mu
        var = _half_sums(d * d) * (1.0 / A_HEAD)
        yn = d * lax.rsqrt(var + GN_EPS_A) * gng_ref[...] + gnb_ref[...]
        ya_ref[rs, :] = ((yn + bonus_ref[rs, :]) * g_ref[rs, :]).astype(BF16)


def _scan(pre, s0, gn_g, gn_b, nb, nj, nck, nseg, carry):
    lkk, lr, rhk, rhb, uk, ub, v, wc, g, bonus = pre
    t = lkk.shape[0]
    rows = nck * WKV_CHUNK
    nst = s0.shape[0] // nb
    blk = pl.BlockSpec((rows, A_WIDTH), lambda i, j: (i * nj + j, 0))
    wcs = pl.BlockSpec((nck * nseg, 1, A_WIDTH), lambda i, j: (i * nj + j, 0, 0))
    sts = pl.BlockSpec((nst, A_PAIRS, LANES, LANES), lambda i, j: (i, 0, 0, 0))
    vec = pl.BlockSpec((1, A_WIDTH), lambda i, j: (0, 0))
    return pl.pallas_call(
        functools.partial(_scan_kernel, nck=nck, nseg=nseg, carry=carry),
        grid=(nb, nj),
        in_specs=[blk] * 7 + [wcs, sts, blk, blk, vec, vec],
        out_specs=[blk, sts],
        out_shape=[jax.ShapeDtypeStruct((t, A_WIDTH), BF16), jax.ShapeDtypeStruct(s0.shape, F32)],
        compiler_params=pltpu.CompilerParams(dimension_semantics=("parallel", "arbitrary"),
                                             vmem_limit_bytes=VMEM_LIMIT),
        name="rwkv_scan",
    )(lkk, lr, rhk, rhb, uk, ub, v, wc, s0, g, bonus, gn_g, gn_b)


def _ret_kernel(q_ref, k_ref, v_ref, gate_ref, cos_ref, sin_ref, din_ref, dq_ref, dk_ref, dc_ref,
                s0_ref, gng_ref, gnb_ref, yb_ref, st_ref, *, nck, nseg, carry):
    ch = RET_CHUNK
    seg = ch // nseg
    if carry:
        @pl.when(pl.program_id(1) == 0)
        def _():
            st_ref[...] = s0_ref[...]
    else:
        st_ref[...] = s0_ref[...]

    lane = lax.broadcasted_iota(jnp.int32, (ch, B_QK), 1)
    first = (lane % B_DK) < (B_DK // 2)
    lane_row = lax.broadcasted_iota(jnp.int32, (1, LANES), 1)
    masks = [jnp.where(lane_row < B_DK, 1.0, 0.0), jnp.where(lane_row < B_DK, 0.0, 1.0)]

    def rotary(x, cos, sin):
        swapped = jnp.where(first, pltpu.roll(x, B_QK - B_DK // 2, axis=1),
                            pltpu.roll(x, B_DK // 2, axis=1))
        return x * cos + swapped * sin

    for c in range(nck):
        rs = slice(c * ch, (c + 1) * ch)
        cos = jnp.concatenate([cos_ref[rs, :]] * (B_QK // LANES), -1)
        sin = jnp.concatenate([sin_ref[rs, :]] * (B_QK // LANES), -1)
        q = rotary(q_ref[rs, :], cos, sin)
        k = rotary(k_ref[rs, :], cos, sin) * (B_DK ** -0.5)
        for h in range(B_HEADS):
            ps = slice((h // 2) * LANES, (h // 2 + 1) * LANES)
            hs = slice(h * B_DV, (h + 1) * B_DV)
            qp = q[:, ps]
            km = k[:, ps] * masks[h % 2]
            vb = v_ref[rs, hs].astype(BF16)
            scores = _dot_nt(qp.astype(BF16), km.astype(BF16)) * din_ref[h]
            o = _dot(scores.astype(BF16), vb)
            qd = qp * dq_ref[h]
            kd = km * dk_ref[h]
            dc = dc_ref[h][0:1, :]
            if nseg == 1:
                st = st_ref[0, h]
                o = o + _dot(qd.astype(BF16), st.astype(BF16))
                st_ref[0, h] = st * dc + _dot_tn(kd.astype(BF16), vb)
            else:
                vf = v_ref[rs, hs]
                cross = []
                for s_ in range(nseg):
                    ss = slice(s_ * seg, (s_ + 1) * seg)
                    i = c * nseg + s_
                    st = st_ref[i, h]
                    cross.append(_dot(qd[ss], st))
                    st_ref[i, h] = st * dc + _dot_tn(kd[ss], vf[ss])
                o = o + jnp.concatenate(cross, 0)
            mu = jnp.mean(o, -1, keepdims=True)
            d = o - mu
            var = jnp.mean(d * d, -1, keepdims=True)
            yn = d * lax.rsqrt(var + GN_EPS_B) * gng_ref[:, hs] + gnb_ref[:, hs]
            gate = gate_ref[rs, hs]
            yb_ref[rs, hs] = (yn * (gate * jax.nn.sigmoid(gate))).astype(BF16)


def _retention(zb, tabs, s0, gn_g, gn_b, nb, nj, nck, nseg, carry, tab_rows_fixed):
    cos, sin, din, dq, dk, dc = tabs
    t = zb.shape[0]
    rows = nck * RET_CHUNK
    nst = s0.shape[0] // nb
    qs = pl.BlockSpec((rows, B_QK), lambda i, j: (i * nj + j, 0))
    ks = pl.BlockSpec((rows, B_QK), lambda i, j: (i * nj + j, 1))
    vs = pl.BlockSpec((rows, B_V), lambda i, j: (i * nj + j, 1))
    gs = pl.BlockSpec((rows, B_V), lambda i, j: (i * nj + j, 2))
    if tab_rows_fixed:
        tab = pl.BlockSpec((rows, LANES), lambda i, j: (0, 0))
    else:
        tab = pl.BlockSpec((rows, LANES), lambda i, j: (j, 0))
    sts = pl.BlockSpec((nst, B_HEADS, LANES, B_DV), lambda i, j: (i, 0, 0, 0))
    const = lambda a: pl.BlockSpec(a.shape, lambda i, j: (0,) * a.ndim)
    return pl.pallas_call(
        functools.partial(_ret_kernel, nck=nck, nseg=nseg, carry=carry),
        grid=(nb, nj),
        in_specs=[qs, ks, vs, gs, tab, tab, const(din), const(dq), const(dk), const(dc), sts,
                  const(gn_g), const(gn_b)],
        out_specs=[pl.BlockSpec((rows, B_V), lambda i, j: (i * nj + j, 0)), sts],
        out_shape=[jax.ShapeDtypeStruct((t, B_V), BF16), jax.ShapeDtypeStruct(s0.shape, F32)],
        compiler_params=pltpu.CompilerParams(dimension_semantics=("parallel", "arbitrary"),
                                             vmem_limit_bytes=VMEM_LIMIT),
        name="retention",
    )(zb, zb, zb, zb, cos, sin, din, dq, dk, dc, s0, gn_g, gn_b)


def _merge_ffn_kernel(x_ref, ya_ref, yb_ref, zg_ref, woa_ref, wob_ref, wo_ref, gpost_ref, gpre_ref,
                      wgate_ref, wup_ref, wdown_ref, gfpost_ref, out_ref):
    zg = zg_ref[...]
    m = (jax.nn.sigmoid(zg[:, :D_MODEL]) * _dot(ya_ref[...], woa_ref[...])
         + jax.nn.sigmoid(zg[:, D_MODEL:]) * _dot(yb_ref[...], wob_ref[...]))
    x1 = x_ref[...] + _rms(_dot(m.astype(BF16), wo_ref[...]), gpost_ref[...])
    hb = _rms(x1, gpre_ref[...]).astype(BF16)
    gate = _dot(hb, wgate_ref[...])
    up = _dot(hb, wup_ref[...])
    f = _dot((gate * jax.nn.sigmoid(gate) * up).astype(BF16), wdown_ref[...])
    out_ref[...] = x1 + _rms(f, gfpost_ref[...])


def _merge_ffn(x, ya, yb, zg, mp, tm):
    t = x.shape[0]
    row = lambda n: pl.BlockSpec((tm, n), lambda i: (i, 0))
    ws = [mp['w_out_a'], mp['w_out_b'], mp['w_o'], mp['g_post'], mp['g_ffn_pre'], mp['w_gate'],
          mp['w_up'], mp['w_down'], mp['g_ffn_post']]
    return pl.pallas_call(
        _merge_ffn_kernel,
        grid=(t // tm,),
        in_specs=[row(D_MODEL), row(A_WIDTH), row(B_V), row(2 * D_MODEL)] +
                 [_resident(w.shape) for w in ws],
        out_specs=row(D_MODEL),
        out_shape=jax.ShapeDtypeStruct((t, D_MODEL), F32),
        compiler_params=pltpu.CompilerParams(dimension_semantics=("parallel",),
                                             vmem_limit_bytes=VMEM_LIMIT),
        name="merge_ffn",
    )(x, ya, yb, zg, *ws)


def _layer_params(l, w_in, mu_shift, w0, lora_w_up, a0, lora_a_up, lora_g_up, k_k, k_a, r_k):
    wi = w_in[l]
    row = lambda a: a.reshape(1, -1)
    wlora = jnp.zeros((LANES, 2 * A_WIDTH), F32)
    wlora = wlora.at[:LORA_W, :A_WIDTH].set(lora_w_up[l]).at[LORA_W:, A_WIDTH:].set(lora_a_up[l])
    lg = jnp.zeros((SHIFT_PAD - 3 * A_WIDTH - LANES, A_WIDTH), F32).at[:LORA_G].set(lora_g_up[l])
    return {
        'wa': jnp.pad(wi[:, :SHIFT_W], ((0, 0), (0, SHIFT_PAD - SHIFT_W))).astype(BF16),
        'wb': wi[:, SHIFT_W:SHIFT_W + RET_W].astype(BF16),
        'wg': wi[:, SHIFT_W + RET_W:].astype(BF16),
        'mu': jnp.pad(row(mu_shift[l]), ((0, 0), (0, SHIFT_PAD - SHIFT_W))),
        'wlora': wlora.astype(BF16), 'lg': lg.astype(BF16),
        'w0': row(w0[l]), 'a0': row(a0[l]), 'k_k': row(k_k[l]), 'k_a': row(k_a[l]), 'r_k': row(r_k[l]),
    }


def _ret_tables(pos, seg, rows):
    half = B_DK // 2
    inv = ROPE_BASE ** (-jnp.arange(half, dtype=F32) / half)
    ang = pos.astype(F32)[:, None] * inv[None, :]
    cos, sin = jnp.cos(ang), jnp.sin(ang)
    cos = jnp.tile(jnp.concatenate([cos, cos], -1), (1, LANES // B_DK))
    sin = jnp.tile(jnp.concatenate([-sin, sin], -1), (1, LANES // B_DK))
    log_g = jnp.log(1.0 - jnp.exp2(-5.0 - jnp.arange(B_HEADS, dtype=F32)))
    idx = jnp.arange(rows)
    loc = (idx % seg).astype(F32)
    rel = loc[:, None] - loc[None, :]
    ok = ((idx[:, None] // seg) == (idx[None, :] // seg)) & (rel >= 0)
    din = jnp.where(ok[None], jnp.exp(log_g[:, None, None] * jnp.where(ok, rel, 0.0)[None]), 0.0)
    bc = lambda col: jnp.broadcast_to(col[:, :, None], (B_HEADS, rows, LANES))
    dq = bc(jnp.exp(log_g[:, None] * (loc[None, :] + 1.0)))
    dk = bc(jnp.exp(log_g[:, None] * (seg - 1.0 - loc[None, :])))
    dc = jnp.broadcast_to(jnp.exp(log_g * seg)[:, None, None], (B_HEADS, 8, LANES))
    return cos, sin, din, dq, dk, dc


def _expand_wkv(s):
    b = s.shape[0]
    s = s.reshape(b, A_PAIRS, 2, A_HEAD, A_HEAD)
    return jnp.einsum('bpivk,ij->bpivjk', s, jnp.eye(2, dtype=s.dtype)).reshape(b, A_PAIRS, LANES, LANES)


def _extract_wkv(s):
    b = s.shape[0]
    s = s.reshape(b, A_PAIRS, 2, A_HEAD, 2, A_HEAD)
    return jnp.stack([s[:, :, 0, :, 0, :], s[:, :, 1, :, 1, :]], 2).reshape(b, A_HEADS, A_HEAD, A_HEAD)


def _expand_ret(s):
    b = s.shape[0]
    s = s.reshape(b, B_HEADS // 2, 2, B_DK, B_DV)
    return jnp.einsum('bpikv,ij->bpijkv', s, jnp.eye(2, dtype=s.dtype)).reshape(b, B_HEADS, LANES, B_DV)


def _extract_ret(s):
    b = s.shape[0]
    s = s.reshape(b, B_HEADS // 2, 2, 2, B_DK, B_DV)
    return jnp.stack([s[:, :, 0, 0], s[:, :, 1, 1]], 2).reshape(b, B_HEADS, B_DK, B_DV)


def _group_layer(x, nseq, seq_len, pos0, states, lp, mp, gn):
    t = x.shape[0]
    long_seq = seq_len >= RET_CHUNK
    tm = 256
    za, zb, zg = _in_proj(x, lp['g_pre'], lp['wa'], lp['wb'], lp['wg'], tm)

    if long_seq:
        wkv_nseg, ret_nseg = 1, 1
        nck_a, nck_b = 4, 2
        prep_rows = 512
        nb, nj_a, nj_b = nseq, seq_len // (nck_a * WKV_CHUNK), seq_len // (nck_b * RET_CHUNK)
        pos = pos0 + jnp.arange(seq_len)
        tabs = _ret_tables(pos, RET_CHUNK, RET_CHUNK)
    else:
        wkv_nseg, ret_nseg = WKV_CHUNK // seq_len, RET_CHUNK // seq_len
        nck_a, nck_b = 1, 1
        prep_rows = RET_CHUNK
        nb, nj_a, nj_b = t // WKV_CHUNK, 1, 1
        pos = pos0 + (jnp.arange(RET_CHUNK) % seq_len)
        tabs = _ret_tables(pos, seq_len, RET_CHUNK)
    nb_b = t // (nck_b * RET_CHUNK * nj_b)

    if states is None:
        s_wkv = jnp.zeros((nseq, A_PAIRS, LANES, LANES), F32)
        s_ret = jnp.zeros((nseq, B_HEADS, LANES, B_DV), F32)
        shift_rows = None
    else:
        s_wkv = _expand_wkv(states[0])
        s_ret = _expand_ret(states[1])
        shift_rows = jnp.repeat(jnp.pad(states[2], ((0, 0), (0, SHIFT_PAD - SHIFT_W))), seq_len, axis=0)

    pre = _prep(za, shift_rows, lp, seq_len, min(seq_len, WKV_CHUNK), prep_rows)
    ya, wkv = _scan(pre, s_wkv, gn['a_g'], gn['a_b'], nb, nj_a, nck_a, wkv_nseg, long_seq)
    yb, ret = _retention(zb, tabs, s_ret, gn['b_g'], gn['b_b'], nb_b, nj_b, nck_b, ret_nseg, long_seq,
                         not long_seq)
    x = _merge_ffn(x, ya, yb, zg, mp, tm)
    shift = za.reshape(nseq, seq_len, SHIFT_PAD)[:, -1, :SHIFT_W]
    return x, _extract_wkv(wkv), _extract_ret(ret), shift


def kernel(x_prompt, x_sample, state_wkv, state_ret, state_shift, norm_mix_pre, w_in, mu_shift, w0,
           lora_w_up, a0, lora_a_up, lora_g_up, k_k, k_a, r_k, gn_a_gain, gn_a_bias, w_out_a,
           gn_b_gain, gn_b_bias, w_out_b, w_o, norm_mix_post, norm_ffn_pre, w_ffn_gate, w_ffn_up,
           w_ffn_down, norm_ffn_post):
    bp, lp_, _ = x_prompt.shape
    bs, ls_, _ = x_sample.shape
    depth = w_in.shape[0]
    yp = x_prompt.reshape(bp * lp_, D_MODEL)
    ys = x_sample.reshape(bs * ls_, D_MODEL)
    row = lambda a: a.reshape(1, -1)
    outs = [[] for _ in range(6)]
    for l in range(depth):
        lp = _layer_params(l, w_in, mu_shift, w0, lora_w_up, a0, lora_a_up, lora_g_up, k_k, k_a, r_k)
        lp['g_pre'] = row(norm_mix_pre[l])
        mp = {'w_out_a': w_out_a[l].astype(BF16), 'w_out_b': w_out_b[l].astype(BF16),
              'w_o': w_o[l].astype(BF16), 'g_post': row(norm_mix_post[l]),
              'g_ffn_pre': row(norm_ffn_pre[l]), 'w_gate': w_ffn_gate[l].astype(BF16),
              'w_up': w_ffn_up[l].astype(BF16), 'w_down': w_ffn_down[l].astype(BF16),
              'g_ffn_post': row(norm_ffn_post[l])}
        gn = {'a_g': row(gn_a_gain[l]), 'a_b': row(gn_a_bias[l]),
              'b_g': row(gn_b_gain[l]), 'b_b': row(gn_b_bias[l])}
        yp, a_, b_, c_ = _group_layer(yp, bp, lp_, 0, None, lp, mp, gn)
        outs[0].append(a_); outs[1].append(b_); outs[2].append(c_)
        ys, a_, b_, c_ = _group_layer(ys, bs, ls_, PAST_LEN,
                                      (state_wkv[l], state_ret[l], state_shift[l]), lp, mp, gn)
        outs[3].append(a_); outs[4].append(b_); outs[5].append(c_)
    return (yp.reshape(bp, lp_, D_MODEL), ys.reshape(bs, ls_, D_MODEL)) + \
        tuple(jnp.stack(o) for o in outs)
```

```python
import functools

import jax
import jax.numpy as jnp
from jax import lax
from jax.experimental import pallas as pl
from jax.experimental.pallas import tpu as pltpu

F32 = jnp.float32
BF16 = jnp.bfloat16

LANES = 128
D_MODEL = 1024
PAST_LEN = 16384
A_HEADS = 8
A_HEAD = 64
A_WIDTH = A_HEADS * A_HEAD
A_PAIRS = A_WIDTH // LANES
LORA_W = 64
LORA_A = 64
LORA_G = 160
SHIFT_W = 3 * A_WIDTH + LORA_W + LORA_A + LORA_G
SHIFT_PAD = 1920
GN_EPS_A = 64e-5
B_HEADS = 8
B_DK = 64
B_DV = 128
B_QK = B_HEADS * B_DK
B_V = B_HEADS * B_DV
RET_W = 2 * B_QK + 2 * B_V
RET_CHUNK = 128
ROPE_BASE = 10000.0
GN_EPS_B = 1e-5
D_FF = 2816
RMS_EPS = 1e-6
WKV_CHUNK = 64

VMEM_LIMIT = 56 * 1024 * 1024


def _resident(shape):
    nd = len(shape)
    return pl.BlockSpec(shape, lambda *_: (0,) * nd, pipeline_mode=pl.Buffered(1))


def _dot(a, b):
    return jnp.dot(a, b, preferred_element_type=F32)


def _dot_nt(a, b):
    return lax.dot_general(a, b, (((1,), (1,)), ((), ())), preferred_element_type=F32)


def _dot_tn(a, b):
    return lax.dot_general(a, b, (((0,), (0,)), ((), ())), preferred_element_type=F32)


def _rms(x, g):
    return x * lax.rsqrt(jnp.mean(x * x, -1, keepdims=True) + RMS_EPS) * g


def _half_sums(x):
    rows, width = x.shape
    lo = lax.broadcasted_iota(jnp.int32, (rows, LANES), 1) < A_HEAD
    out = []
    for p in range(width // LANES):
        xp = x[:, p * LANES:(p + 1) * LANES]
        s_lo = jnp.sum(jnp.where(lo, xp, 0.0), -1, keepdims=True)
        s_hi = jnp.sum(jnp.where(lo, 0.0, xp), -1, keepdims=True)
        out.append(jnp.where(lo, s_lo, s_hi))
    return jnp.concatenate(out, -1)


def _in_proj_kernel(x_ref, g_ref, wa_ref, wb_ref, wg_ref, za_ref, zb_ref, zg_ref):
    hb = _rms(x_ref[...], g_ref[...]).astype(BF16)
    za_ref[...] = _dot(hb, wa_ref[...])
    zb_ref[...] = _dot(hb, wb_ref[...])
    zg_ref[...] = _dot(hb, wg_ref[...])


def _in_proj(x, g, wa, wb, wg, tm):
    t = x.shape[0]
    row = lambda n: pl.BlockSpec((tm, n), lambda i: (i, 0))
    return pl.pallas_call(
        _in_proj_kernel,
        grid=(t // tm,),
        in_specs=[row(D_MODEL), _resident(g.shape), _resident(wa.shape), _resident(wb.shape),
                  _resident(wg.shape)],
        out_specs=[row(SHIFT_PAD), row(RET_W), row(2 * D_MODEL)],
        out_shape=[jax.ShapeDtypeStruct((t, SHIFT_PAD), F32), jax.ShapeDtypeStruct((t, RET_W), F32),
                   jax.ShapeDtypeStruct((t, 2 * D_MODEL), F32)],
        compiler_params=pltpu.CompilerParams(dimension_semantics=("parallel",),
                                             vmem_limit_bytes=VMEM_LIMIT),
        name="in_proj",
    )(x, g, wa, wb, wg)


def _softplus(y):
    return jnp.maximum(y, 0.0) + jnp.log(1.0 + jnp.exp(-jnp.abs(y)))


def _prep_kernel(*refs, seq_len, seg, has_shift):
    if has_shift:
        za_ref, halo_ref, shift_ref = refs[:3]
        refs = refs[3:]
    else:
        za_ref, halo_ref = refs[:2]
        shift_ref = None
        refs = refs[2:]
    (mu_ref, wlora_ref, lg_ref, w0_ref, a0_ref, kk_ref, ka_ref, rk_ref,
     lkk_ref, lr_ref, rhk_ref, rhb_ref, uk_ref, ub_ref, v_ref, wc_ref, g_ref, bonus_ref) = refs

    za = za_ref[...]
    rows = za.shape[0]
    row = lax.broadcasted_iota(jnp.int32, (rows, 1), 0)
    prev = pltpu.roll(za, 1, axis=0)
    prev = jnp.where(row == 0, halo_ref[7:8, :], prev)
    if has_shift:
        prev = jnp.where(row % seq_len == 0, shift_ref[...], prev)
    else:
        starts = (pl.program_id(0) * rows) % seq_len == 0
        prev = jnp.where((row == 0) & starts, 0.0, prev)
    zs = za + mu_ref[...] * (prev - za)

    r = zs[:, 0:A_WIDTH]
    k = zs[:, A_WIDTH:2 * A_WIDTH]
    v = zs[:, 2 * A_WIDTH:3 * A_WIDTH]
    dwa = zs[:, 3 * A_WIDTH:3 * A_WIDTH + LANES]
    dg = zs[:, 3 * A_WIDTH + LANES:SHIFT_PAD]
    lane = lax.broadcasted_iota(jnp.int32, (rows, LANES), 1)
    act = jnp.where(lane < LORA_W, jnp.tanh(dwa), dwa).astype(BF16)
    lora = _dot(act, wlora_ref[...])
    w = -_softplus(-(w0_ref[...] + lora[:, :A_WIDTH])) - 0.5
    lw = -jnp.exp(w)
    a = jax.nn.sigmoid(a0_ref[...] + lora[:, A_WIDTH:])
    g_ref[...] = _dot(jax.nn.sigmoid(dg).astype(BF16), lg_ref[...])

    kk = k * kk_ref[...]
    kk = kk * lax.rsqrt(jnp.maximum(_half_sums(kk * kk), 1e-24))
    k2 = k * (1.0 + (a - 1.0) * ka_ref[...])
    b = kk * a
    bonus_ref[...] = _half_sums(r * k2 * rk_ref[...]) * v
    v_ref[...] = v.astype(BF16)

    ti = lax.broadcasted_iota(jnp.int32, (LANES, LANES), 0)
    tj = lax.broadcasted_iota(jnp.int32, (LANES, LANES), 1)
    same = (ti // seg) == (tj // seg)
    tri = jnp.concatenate([jnp.where(same & (tj <= ti), 1.0, 0.0),
                           jnp.where(same, 1.0, 0.0)], 0).astype(BF16)
    for g0 in range(0, rows, LANES):
        sl = slice(g0, g0 + LANES)
        x = lw[sl]
        hi = x.astype(BF16)
        r1 = x - hi.astype(F32)
        mid = r1.astype(BF16)
        lo = (r1 - mid.astype(F32)).astype(BF16)
        ct = _dot(tri, hi) + _dot(tri, mid) + _dot(tri, lo)
        cum, tot = ct[:LANES], ct[LANES:]
        inv = jnp.exp(-cum)
        tail = jnp.exp(tot - cum)
        lkk_ref[sl, :] = (kk[sl] * jnp.exp(cum - x)).astype(BF16)
        lr_ref[sl, :] = (r[sl] * jnp.exp(cum)).astype(BF16)
        rhk_ref[sl, :] = (k2[sl] * inv).astype(BF16)
        rhb_ref[sl, :] = (b[sl] * inv).astype(BF16)
        uk_ref[sl, :] = (k2[sl] * tail).astype(BF16)
        ub_ref[sl, :] = (-(b[sl] * tail)).astype(BF16)
        for s in range(LANES // seg):
            wc_ref[g0 // seg + s] = jnp.exp(tot[s * seg:s * seg + 1, :])


def _prep(za, shift_rows, pp, seq_len, seg, rows):
    t = za.shape[0]
    has_shift = shift_rows is not None
    blk = lambda n: pl.BlockSpec((rows, n), lambda i: (i, 0))
    halo = pl.BlockSpec((8, SHIFT_PAD), lambda i: (jnp.maximum(i * (rows // 8) - 1, 0), 0))
    params = [pp['mu'], pp['wlora'], pp['lg'], pp['w0'], pp['a0'], pp['k_k'], pp['k_a'], pp['r_k']]
    ins = [za, za] + ([shift_rows] if has_shift else []) + params
    in_specs = [blk(SHIFT_PAD), halo] + ([blk(SHIFT_PAD)] if has_shift else []) + \
        [_resident(p.shape) for p in params]
    bf = jax.ShapeDtypeStruct((t, A_WIDTH), BF16)
    f32 = jax.ShapeDtypeStruct((t, A_WIDTH), F32)
    out_shape = [bf] * 7 + [jax.ShapeDtypeStruct((t // seg, 1, A_WIDTH), F32), f32, f32]
    out_specs = [blk(A_WIDTH)] * 7 + [pl.BlockSpec((rows // seg, 1, A_WIDTH), lambda i: (i, 0, 0)),
                                      blk(A_WIDTH), blk(A_WIDTH)]
    return pl.pallas_call(
        functools.partial(_prep_kernel, seq_len=seq_len, seg=seg, has_shift=has_shift),
        grid=(t // rows,),
        in_specs=in_specs, out_specs=out_specs, out_shape=out_shape,
        compiler_params=pltpu.CompilerParams(dimension_semantics=("parallel",),
                                             vmem_limit_bytes=VMEM_LIMIT),
        name="rwkv_prep",
    )(*ins)


def _scan_seq_kernel(lkk_ref, lr_ref, rhk_ref, rhb_ref, uk_ref, ub_ref, v_ref, wc_ref, s0_ref,
                     g_ref, bonus_ref, gng_ref, gnb_ref, ya_ref, st_ref,
                     xlk_s, xavt_s, arb_s, yv_s, uv_s, *, nck):
    c64 = WKV_CHUNK

    @pl.when(pl.program_id(1) == 0)
    def _():
        st_ref[...] = s0_ref[...]

    ti = lax.broadcasted_iota(jnp.int32, (LANES, LANES), 0)
    tj = lax.broadcasted_iota(jnp.int32, (LANES, LANES), 1)
    same_head = (ti // c64) == (tj // c64)
    strict = same_head & (tj < ti)
    incl = same_head & (tj <= ti)
    eye = jnp.where(ti == tj, 1.0, 0.0)
    level = lambda s: ((ti // s) % 2 == 1) & ((tj // s) == (ti // s) - 1)
    levels = [2 ** e for e in range(1, 6)]
    level_b = {s: jnp.where(level(s), 1.0, 0.0).astype(BF16) for s in levels}
    lane_row = lax.broadcasted_iota(jnp.int32, (1, LANES), 1)
    m_lo = jnp.where(lane_row < A_HEAD, 1.0, 0.0).astype(BF16)
    m_hi = jnp.where(lane_row < A_HEAD, 0.0, 1.0).astype(BF16)
    lo_half = lax.broadcasted_iota(jnp.int32, (c64, LANES), 1) < A_HEAD
    stack_heads = lambda x: jnp.concatenate([x * m_lo, x * m_hi], 0)
    twice = lambda x: jnp.concatenate([x, x], 0)
    unstack = lambda x: jnp.where(lo_half, x[:c64], x[c64:])
    bf = lambda x: x.astype(BF16)

    units = [(c, p) for c in range(nck) for p in range(A_PAIRS)]
    group = 8
    for g0 in range(0, len(units), group):
        grp = [g0 + i for i in range(len(units[g0:g0 + group]))]
        sl = {u: (slice(units[u][0] * c64, (units[u][0] + 1) * c64),
                  slice(units[u][1] * LANES, (units[u][1] + 1) * LANES)) for u in grp}
        akk, akb, ark, xs = {}, {}, {}, {}
        for u in grp:
            rs, cs = sl[u]
            a4 = _dot_nt(jnp.concatenate([stack_heads(lkk_ref[rs, cs]), stack_heads(lr_ref[rs, cs])], 0),
                         jnp.concatenate([twice(rhk_ref[rs, cs]), twice(rhb_ref[rs, cs])], 0))
            akk[u] = bf(jnp.where(strict, a4[:LANES, :LANES], 0.0))
            a_kb = jnp.where(strict, a4[:LANES, LANES:], 0.0)
            ark[u] = bf(jnp.where(incl, a4[LANES:, :LANES], 0.0))
            arb_s[u] = bf(jnp.where(incl, a4[LANES:, LANES:], 0.0))
            akb[u] = bf(a_kb)
            xs[u] = eye - jnp.where(level(1), a_kb, 0.0)
        for s in levels:
            xb = {u: bf(xs[u]) for u in grp}
            t1 = {u: bf(_dot(xb[u], akb[u] * level_b[s])) for u in grp}
            xs = {u: xs[u] - _dot(t1[u], xb[u]) for u in grp}
        xb = {u: bf(xs[u]) for u in grp}
        vv = {u: twice(v_ref[sl[u]]) for u in grp}
        for u in grp:
            xlk_s[u] = bf(_dot(xb[u], twice(lkk_ref[sl[u]])))
        av = {u: bf(_dot(akk[u], vv[u])) for u in grp}
        for u in grp:
            xavt_s[u] = _dot(xb[u], av[u]).T
        for u in grp:
            yv_s[u] = unstack(_dot(ark[u], vv[u]))
        for u in grp:
            uv_s[u] = jnp.where(same_head, _dot_tn(v_ref[sl[u]], uk_ref[sl[u]]), 0.0)

    pairs = range(A_PAIRS)
    cols = [slice(p * LANES, (p + 1) * LANES) for p in pairs]
    for c in range(nck):
        rs = slice(c * c64, (c + 1) * c64)
        us = [c * A_PAIRS + p for p in pairs]
        st = [st_ref[0, p] for p in pairs]
        stb = [bf(s_) for s_ in st]
        skt = [bf(jnp.where(same_head, _dot_nt(stb[p], xlk_s[us[p]]) + xavt_s[us[p]], 0.0))
               for p in pairs]
        upd = [_dot(skt[p], stack_heads(ub_ref[rs, cols[p]])) for p in pairs]
        for p in pairs:
            st_ref[0, p] = st[p] * wc_ref[c, :, cols[p]] + uv_s[us[p]] + upd[p]
        g_r = [_dot_nt(lr_ref[rs, cols[p]], stb[p]) for p in pairs]
        y2 = [_dot_nt(arb_s[us[p]], skt[p]) for p in pairs]
        y = jnp.concatenate([g_r[p] + yv_s[us[p]] - (y2[p][:c64] + y2[p][c64:]) for p in pairs], -1)
        mu = _half_sums(y) * (1.0 / A_HEAD)
        d = y - mu
        var = _half_sums(d * d) * (1.0 / A_HEAD)
        yn = d * lax.rsqrt(var + GN_EPS_A) * gng_ref[...] + gnb_ref[...]
        ya_ref[rs, :] = ((yn + bonus_ref[rs, :]) * g_ref[rs, :]).astype(BF16)


def _scan_multi_kernel(lkk_ref, lr_ref, rhk_ref, rhb_ref, uk_ref, ub_ref, v_ref, wc_ref, s0_ref,
                       g_ref, bonus_ref, gng_ref, gnb_ref, ya_ref, st_ref, *, nck, nseg, carry):
    c64 = WKV_CHUNK
    seg = c64 // nseg
    if carry:
        @pl.when(pl.program_id(1) == 0)
        def _():
            st_ref[...] = s0_ref[...]
    else:
        st_ref[...] = s0_ref[...]

    ti = lax.broadcasted_iota(jnp.int32, (LANES, LANES), 0)
    tj = lax.broadcasted_iota(jnp.int32, (LANES, LANES), 1)
    same_head = (ti // c64) == (tj // c64)
    same_seg = (ti // seg) == (tj // seg)
    strict = same_seg & (tj < ti)
    incl = same_seg & (tj <= ti)
    eye = jnp.where(ti == tj, 1.0, 0.0)
    lane_row = lax.broadcasted_iota(jnp.int32, (1, LANES), 1)
    m_lo = jnp.where(lane_row < A_HEAD, 1.0, 0.0).astype(BF16)
    m_hi = jnp.where(lane_row < A_HEAD, 0.0, 1.0).astype(BF16)
    lo_half = lax.broadcasted_iota(jnp.int32, (c64, LANES), 1) < A_HEAD
    stack_heads = lambda x: jnp.concatenate([x * m_lo, x * m_hi], 0)
    twice = lambda x: jnp.concatenate([x, x], 0)
    unstack = lambda x: jnp.where(lo_half, x[:c64], x[c64:])

    for c in range(nck):
        rs = slice(c * c64, (c + 1) * c64)
        y_pairs = []
        for p in range(A_PAIRS):
            cs = slice(p * LANES, (p + 1) * LANES)
            lk, lr = lkk_ref[rs, cs], lr_ref[rs, cs]
            rk2, rb2 = twice(rhk_ref[rs, cs]), twice(rhb_ref[rs, cs])
            v = v_ref[rs, cs]
            lkm, lrm = stack_heads(lk), stack_heads(lr)
            a_kk = jnp.where(strict, _dot_nt(lkm, rk2), 0.0)
            a_kb = jnp.where(strict, _dot_nt(lkm, rb2), 0.0)
            a_rk = jnp.where(incl, _dot_nt(lrm, rk2), 0.0)
            a_rb = jnp.where(incl, _dot_nt(lrm, rb2), 0.0)

            x = eye
            s = 1
            while s < seg:
                lvl = ((ti // s) % 2 == 1) & ((tj // s) == (ti // s) - 1)
                xb = x.astype(BF16)
                x = x - _dot(_dot(xb, jnp.where(lvl, a_kb, 0.0).astype(BF16)).astype(BF16), xb)
                s *= 2

            if nseg == 1:
                st = st_ref[0, p]
                g_k = _dot_nt(lk, st.astype(BF16))
                g_r = _dot_nt(lr, st.astype(BF16))
            else:
                lkf, lrf = lk.astype(F32), lr.astype(F32)
                gk, gr = [], []
                for s_ in range(nseg):
                    ss = slice(s_ * seg, (s_ + 1) * seg)
                    st = st_ref[c * nseg + s_, p]
                    gk.append(_dot_nt(lkf[ss], st))
                    gr.append(_dot_nt(lrf[ss], st))
                g_k, g_r = jnp.concatenate(gk, 0), jnp.concatenate(gr, 0)

            rhs = twice(g_k) + _dot(a_kk.astype(BF16), twice(v))
            sk = unstack(_dot(x.astype(BF16), rhs.astype(BF16)))
            skb = sk.astype(BF16)
            y2 = _dot(a_rk.astype(BF16), twice(v)) - _dot(a_rb.astype(BF16), twice(skb))
            y_pairs.append(g_r + unstack(y2))

            uk, ub = uk_ref[rs, cs], ub_ref[rs, cs]
            if nseg == 1:
                upd = _dot_tn(jnp.concatenate([v, skb], 0), jnp.concatenate([uk, ub], 0))
                st_ref[0, p] = st_ref[0, p] * wc_ref[c, :, cs] + jnp.where(same_head, upd, 0.0)
            else:
                vf, ukf, ubf = v.astype(F32), uk.astype(F32), ub.astype(F32)
                for s_ in range(nseg):
                    ss = slice(s_ * seg, (s_ + 1) * seg)
                    upd = _dot_tn(jnp.concatenate([vf[ss], sk[ss]], 0),
                                  jnp.concatenate([ukf[ss], ubf[ss]], 0))
                    i = c * nseg + s_
                    st_ref[i, p] = st_ref[i, p] * wc_ref[i, :, cs] + jnp.where(same_head, upd, 0.0)

        y = jnp.concatenate(y_pairs, -1)
        mu = _half_sums(y) * (1.0 / A_HEAD)
        d = y - mu
        var = _half_sums(d * d) * (1.0 / A_HEAD)
        yn = d * lax.rsqrt(var + GN_EPS_A) * gng_ref[...] + gnb_ref[...]
        ya_ref[rs, :] = ((yn + bonus_ref[rs, :]) * g_ref[rs, :]).astype(BF16)


def _scan(pre, s0, gn_g, gn_b, nb, nj, nck, nseg, carry):
    lkk, lr, rhk, rhb, uk, ub, v, wc, g, bonus = pre
    t = lkk.shape[0]
    rows = nck * WKV_CHUNK
    nst = s0.shape[0] // nb
    blk = pl.BlockSpec((rows, A_WIDTH), lambda i, j: (i * nj + j, 0))
    wcs = pl.BlockSpec((nck * nseg, 1, A_WIDTH), lambda i, j: (i * nj + j, 0, 0))
    sts = pl.BlockSpec((nst, A_PAIRS, LANES, LANES), lambda i, j: (i, 0, 0, 0))
    vec = pl.BlockSpec((1, A_WIDTH), lambda i, j: (0, 0))
    if carry:
        units = nck * A_PAIRS
        body = functools.partial(_scan_seq_kernel, nck=nck)
        scratch = [pltpu.VMEM((units, LANES, LANES), BF16), pltpu.VMEM((units, LANES, LANES), F32),
                   pltpu.VMEM((units, LANES, LANES), BF16), pltpu.VMEM((units, WKV_CHUNK, LANES), F32),
                   pltpu.VMEM((units, LANES, LANES), F32)]
    else:
        body = functools.partial(_scan_multi_kernel, nck=nck, nseg=nseg, carry=carry)
        scratch = []
    return pl.pallas_call(
        body,
        grid=(nb, nj),
        in_specs=[blk] * 7 + [wcs, sts, blk, blk, vec, vec],
        out_specs=[blk, sts],
        scratch_shapes=scratch,
        out_shape=[jax.ShapeDtypeStruct((t, A_WIDTH), BF16), jax.ShapeDtypeStruct(s0.shape, F32)],
        compiler_params=pltpu.CompilerParams(dimension_semantics=("parallel", "arbitrary"),
                                             vmem_limit_bytes=VMEM_LIMIT),
        name="rwkv_scan",
    )(lkk, lr, rhk, rhb, uk, ub, v, wc, s0, g, bonus, gn_g, gn_b)


def _ret_kernel(q_ref, k_ref, v_ref, gate_ref, cos_ref, sin_ref, din_ref, dq_ref, dk_ref, dc_ref,
                s0_ref, gng_ref, gnb_ref, yb_ref, st_ref, *, nck, nseg, carry):
    ch = RET_CHUNK
    seg = ch // nseg
    if carry:
        @pl.when(pl.program_id(1) == 0)
        def _():
            st_ref[...] = s0_ref[...]
    else:
        st_ref[...] = s0_ref[...]

    lane = lax.broadcasted_iota(jnp.int32, (ch, B_QK), 1)
    first = (lane % B_DK) < (B_DK // 2)
    lane_row = lax.broadcasted_iota(jnp.int32, (1, LANES), 1)
    masks = [jnp.where(lane_row < B_DK, 1.0, 0.0), jnp.where(lane_row < B_DK, 0.0, 1.0)]

    def rotary(x, cos, sin):
        swapped = jnp.where(first, pltpu.roll(x, B_QK - B_DK // 2, axis=1),
                            pltpu.roll(x, B_DK // 2, axis=1))
        return x * cos + swapped * sin

    for c in range(nck):
        rs = slice(c * ch, (c + 1) * ch)
        cos = jnp.concatenate([cos_ref[rs, :]] * (B_QK // LANES), -1)
        sin = jnp.concatenate([sin_ref[rs, :]] * (B_QK // LANES), -1)
        q = rotary(q_ref[rs, :], cos, sin)
        k = rotary(k_ref[rs, :], cos, sin) * (B_DK ** -0.5)
        for h in range(B_HEADS):
            ps = slice((h // 2) * LANES, (h // 2 + 1) * LANES)
            hs = slice(h * B_DV, (h + 1) * B_DV)
            qp = q[:, ps]
            km = k[:, ps] * masks[h % 2]
            vb = v_ref[rs, hs].astype(BF16)
            scores = _dot_nt(qp.astype(BF16), km.astype(BF16)) * din_ref[h]
            o = _dot(scores.astype(BF16), vb)
            qd = qp * dq_ref[h]
            kd = km * dk_ref[h]
            dc = dc_ref[h][0:1, :]
            if nseg == 1:
                st = st_ref[0, h]
                o = o + _dot(qd.astype(BF16), st.astype(BF16))
                st_ref[0, h] = st * dc + _dot_tn(kd.astype(BF16), vb)
            else:
                vf = v_ref[rs, hs]
                cross = []
                for s_ in range(nseg):
                    ss = slice(s_ * seg, (s_ + 1) * seg)
                    i = c * nseg + s_
                    st = st_ref[i, h]
                    cross.append(_dot(qd[ss], st))
                    st_ref[i, h] = st * dc + _dot_tn(kd[ss], vf[ss])
                o = o + jnp.concatenate(cross, 0)
            mu = jnp.mean(o, -1, keepdims=True)
            d = o - mu
            var = jnp.mean(d * d, -1, keepdims=True)
            yn = d * lax.rsqrt(var + GN_EPS_B) * gng_ref[:, hs] + gnb_ref[:, hs]
            gate = gate_ref[rs, hs]
            yb_ref[rs, hs] = (yn * (gate * jax.nn.sigmoid(gate))).astype(BF16)


def _retention(zb, tabs, s0, gn_g, gn_b, nb, nj, nck, nseg, carry, tab_rows_fixed):
    cos, sin, din, dq, dk, dc = tabs
    t = zb.shape[0]
    rows = nck * RET_CHUNK
    nst = s0.shape[0] // nb
    qs = pl.BlockSpec((rows, B_QK), lambda i, j: (i * nj + j, 0))
    ks = pl.BlockSpec((rows, B_QK), lambda i, j: (i * nj + j, 1))
    vs = pl.BlockSpec((rows, B_V), lambda i, j: (i * nj + j, 1))
    gs = pl.BlockSpec((rows, B_V), lambda i, j: (i * nj + j, 2))
    if tab_rows_fixed:
        tab = pl.BlockSpec((rows, LANES), lambda i, j: (0, 0))
    else:
        tab = pl.BlockSpec((rows, LANES), lambda i, j: (j, 0))
    sts = pl.BlockSpec((nst, B_HEADS, LANES, B_DV), lambda i, j: (i, 0, 0, 0))
    const = lambda a: pl.BlockSpec(a.shape, lambda i, j: (0,) * a.ndim)
    return pl.pallas_call(
        functools.partial(_ret_kernel, nck=nck, nseg=nseg, carry=carry),
        grid=(nb, nj),
        in_specs=[qs, ks, vs, gs, tab, tab, const(din), const(dq), const(dk), const(dc), sts,
                  const(gn_g), const(gn_b)],
        out_specs=[pl.BlockSpec((rows, B_V), lambda i, j: (i * nj + j, 0)), sts],
        out_shape=[jax.ShapeDtypeStruct((t, B_V), BF16), jax.ShapeDtypeStruct(s0.shape, F32)],
        compiler_params=pltpu.CompilerParams(dimension_semantics=("parallel", "arbitrary"),
                                             vmem_limit_bytes=VMEM_LIMIT),
        name="retention",
    )(zb, zb, zb, zb, cos, sin, din, dq, dk, dc, s0, gn_g, gn_b)


def _merge_ffn_kernel(x_ref, ya_ref, yb_ref, zg_ref, woa_ref, wob_ref, wo_ref, gpost_ref, gpre_ref,
                      wgate_ref, wup_ref, wdown_ref, gfpost_ref, out_ref):
    zg = zg_ref[...]
    m = (jax.nn.sigmoid(zg[:, :D_MODEL]) * _dot(ya_ref[...], woa_ref[...])
         + jax.nn.sigmoid(zg[:, D_MODEL:]) * _dot(yb_ref[...], wob_ref[...]))
    x1 = x_ref[...] + _rms(_dot(m.astype(BF16), wo_ref[...]), gpost_ref[...])
    hb = _rms(x1, gpre_ref[...]).astype(BF16)
    gate = _dot(hb, wgate_ref[...])
    up = _dot(hb, wup_ref[...])
    f = _dot((gate * jax.nn.sigmoid(gate) * up).astype(BF16), wdown_ref[...])
    out_ref[...] = x1 + _rms(f, gfpost_ref[...])


def _merge_ffn(x, ya, yb, zg, mp, tm):
    t = x.shape[0]
    row = lambda n: pl.BlockSpec((tm, n), lambda i: (i, 0))
    ws = [mp['w_out_a'], mp['w_out_b'], mp['w_o'], mp['g_post'], mp['g_ffn_pre'], mp['w_gate'],
          mp['w_up'], mp['w_down'], mp['g_ffn_post']]
    return pl.pallas_call(
        _merge_ffn_kernel,
        grid=(t // tm,),
        in_specs=[row(D_MODEL), row(A_WIDTH), row(B_V), row(2 * D_MODEL)] +
                 [_resident(w.shape) for w in ws],
        out_specs=row(D_MODEL),
        out_shape=jax.ShapeDtypeStruct((t, D_MODEL), F32),
        compiler_params=pltpu.CompilerParams(dimension_semantics=("parallel",),
                                             vmem_limit_bytes=VMEM_LIMIT),
        name="merge_ffn",
    )(x, ya, yb, zg, *ws)


def _layer_params(l, w_in, mu_shift, w0, lora_w_up, a0, lora_a_up, lora_g_up, k_k, k_a, r_k):
    wi = w_in[l]
    row = lambda a: a.reshape(1, -1)
    wlora = jnp.zeros((LANES, 2 * A_WIDTH), F32)
    wlora = wlora.at[:LORA_W, :A_WIDTH].set(lora_w_up[l]).at[LORA_W:, A_WIDTH:].set(lora_a_up[l])
    lg = jnp.zeros((SHIFT_PAD - 3 * A_WIDTH - LANES, A_WIDTH), F32).at[:LORA_G].set(lora_g_up[l])
    return {
        'wa': jnp.pad(wi[:, :SHIFT_W], ((0, 0), (0, SHIFT_PAD - SHIFT_W))).astype(BF16),
        'wb': wi[:, SHIFT_W:SHIFT_W + RET_W].astype(BF16),
        'wg': wi[:, SHIFT_W + RET_W:].astype(BF16),
        'mu': jnp.pad(row(mu_shift[l]), ((0, 0), (0, SHIFT_PAD - SHIFT_W))),
        'wlora': wlora.astype(BF16), 'lg': lg.astype(BF16),
        'w0': row(w0[l]), 'a0': row(a0[l]), 'k_k': row(k_k[l]), 'k_a': row(k_a[l]), 'r_k': row(r_k[l]),
    }


def _ret_tables(pos, seg, rows):
    half = B_DK // 2
    inv = ROPE_BASE ** (-jnp.arange(half, dtype=F32) / half)
    ang = pos.astype(F32)[:, None] * inv[None, :]
    cos, sin = jnp.cos(ang), jnp.sin(ang)
    cos = jnp.tile(jnp.concatenate([cos, cos], -1), (1, LANES // B_DK))
    sin = jnp.tile(jnp.concatenate([-sin, sin], -1), (1, LANES // B_DK))
    log_g = jnp.log(1.0 - jnp.exp2(-5.0 - jnp.arange(B_HEADS, dtype=F32)))
    idx = jnp.arange(rows)
    loc = (idx % seg).astype(F32)
    rel = loc[:, None] - loc[None, :]
    ok = ((idx[:, None] // seg) == (idx[None, :] // seg)) & (rel >= 0)
    din = jnp.where(ok[None], jnp.exp(log_g[:, None, None] * jnp.where(ok, rel, 0.0)[None]), 0.0)
    bc = lambda col: jnp.broadcast_to(col[:, :, None], (B_HEADS, rows, LANES))
    dq = bc(jnp.exp(log_g[:, None] * (loc[None, :] + 1.0)))
    dk = bc(jnp.exp(log_g[:, None] * (seg - 1.0 - loc[None, :])))
    dc = jnp.broadcast_to(jnp.exp(log_g * seg)[:, None, None], (B_HEADS, 8, LANES))
    return cos, sin, din, dq, dk, dc


def _expand_wkv(s):
    b = s.shape[0]
    s = s.reshape(b, A_PAIRS, 2, A_HEAD, A_HEAD)
    return jnp.einsum('bpivk,ij->bpivjk', s, jnp.eye(2, dtype=s.dtype)).reshape(b, A_PAIRS, LANES, LANES)


def _extract_wkv(s):
    b = s.shape[0]
    s = s.reshape(b, A_PAIRS, 2, A_HEAD, 2, A_HEAD)
    return jnp.stack([s[:, :, 0, :, 0, :], s[:, :, 1, :, 1, :]], 2).reshape(b, A_HEADS, A_HEAD, A_HEAD)


def _expand_ret(s):
    b = s.shape[0]
    s = s.reshape(b, B_HEADS // 2, 2, B_DK, B_DV)
    return jnp.einsum('bpikv,ij->bpijkv', s, jnp.eye(2, dtype=s.dtype)).reshape(b, B_HEADS, LANES, B_DV)


def _extract_ret(s):
    b = s.shape[0]
    s = s.reshape(b, B_HEADS // 2, 2, 2, B_DK, B_DV)
    return jnp.stack([s[:, :, 0, 0], s[:, :, 1, 1]], 2).reshape(b, B_HEADS, B_DK, B_DV)


def _group_layer(x, nseq, seq_len, pos0, states, lp, mp, gn):
    t = x.shape[0]
    long_seq = seq_len >= RET_CHUNK
    tm = 256
    za, zb, zg = _in_proj(x, lp['g_pre'], lp['wa'], lp['wb'], lp['wg'], tm)

    if long_seq:
        wkv_nseg, ret_nseg = 1, 1
        nck_a, nck_b = 4, 2
        prep_rows = 512
        nb, nj_a, nj_b = nseq, seq_len // (nck_a * WKV_CHUNK), seq_len // (nck_b * RET_CHUNK)
        pos = pos0 + jnp.arange(seq_len)
        tabs = _ret_tables(pos, RET_CHUNK, RET_CHUNK)
    else:
        wkv_nseg, ret_nseg = WKV_CHUNK // seq_len, RET_CHUNK // seq_len
        nck_a, nck_b = 1, 1
        prep_rows = RET_CHUNK
        nb, nj_a, nj_b = t // WKV_CHUNK, 1, 1
        pos = pos0 + (jnp.arange(RET_CHUNK) % seq_len)
        tabs = _ret_tables(pos, seq_len, RET_CHUNK)
    nb_b = t // (nck_b * RET_CHUNK * nj_b)

    if states is None:
        s_wkv = jnp.zeros((nseq, A_PAIRS, LANES, LANES), F32)
        s_ret = jnp.zeros((nseq, B_HEADS, LANES, B_DV), F32)
        shift_rows = None
    else:
        s_wkv = _expand_wkv(states[0])
        s_ret = _expand_ret(states[1])
        shift_rows = jnp.repeat(jnp.pad(states[2], ((0, 0), (0, SHIFT_PAD - SHIFT_W))), seq_len, axis=0)

    pre = _prep(za, shift_rows, lp, seq_len, min(seq_len, WKV_CHUNK), prep_rows)
    ya, wkv = _scan(pre, s_wkv, gn['a_g'], gn['a_b'], nb, nj_a, nck_a, wkv_nseg, long_seq)
    yb, ret = _retention(zb, tabs, s_ret, gn['b_g'], gn['b_b'], nb_b, nj_b, nck_b, ret_nseg, long_seq,
                         not long_seq)
    x = _merge_ffn(x, ya, yb, zg, mp, tm)
    shift = za.reshape(nseq, seq_len, SHIFT_PAD)[:, -1, :SHIFT_W]
    return x, _extract_wkv(wkv), _extract_ret(ret), shift


def kernel(x_prompt, x_sample, state_wkv, state_ret, state_shift, norm_mix_pre, w_in, mu_shift, w0,
           lora_w_up, a0, lora_a_up, lora_g_up, k_k, k_a, r_k, gn_a_gain, gn_a_bias, w_out_a,
           gn_b_gain, gn_b_bias, w_out_b, w_o, norm_mix_post, norm_ffn_pre, w_ffn_gate, w_ffn_up,
           w_ffn_down, norm_ffn_post):
    bp, lp_, _ = x_prompt.shape
    bs, ls_, _ = x_sample.shape
    depth = w_in.shape[0]
    yp = x_prompt.reshape(bp * lp_, D_MODEL)
    ys = x_sample.reshape(bs * ls_, D_MODEL)
    row = lambda a: a.reshape(1, -1)
    outs = [[] for _ in range(6)]
    for l in range(depth):
        lp = _layer_params(l, w_in, mu_shift, w0, lora_w_up, a0, lora_a_up, lora_g_up, k_k, k_a, r_k)
        lp['g_pre'] = row(norm_mix_pre[l])
        mp = {'w_out_a': w_out_a[l].astype(BF16), 'w_out_b': w_out_b[l].astype(BF16),
              'w_o': w_o[l].astype(BF16), 'g_post': row(norm_mix_post[l]),
              'g_ffn_pre': row(norm_ffn_pre[l]), 'w_gate': w_ffn_gate[l].astype(BF16),
              'w_up': w_ffn_up[l].astype(BF16), 'w_down': w_ffn_down[l].astype(BF16),
              'g_ffn_post': row(norm_ffn_post[l])}
        gn = {'a_g': row(gn_a_gain[l]), 'a_b': row(gn_a_bias[l]),
              'b_g': row(gn_b_gain[l]), 'b_b': row(gn_b_bias[l])}
        yp, a_, b_, c_ = _group_layer(yp, bp, lp_, 0, None, lp, mp, gn)
        outs[0].append(a_); outs[1].append(b_); outs[2].append(c_)
        ys, a_, b_, c_ = _group_layer(ys, bs, ls_, PAST_LEN,
                                      (state_wkv[l], state_ret[l], state_shift[l]), lp, mp, gn)
        outs[3].append(a_); outs[4].append(b_); outs[5].append(c_)
    return (yp.reshape(bp, lp_, D_MODEL), ys.reshape(bs, ls_, D_MODEL)) + \
        tuple(jnp.stack(o) for o in outs)
```

```python
import functools

import jax
import jax.numpy as jnp
from jax import lax
from jax.experimental import pallas as pl
from jax.experimental.pallas import tpu as pltpu

F32 = jnp.float32
BF16 = jnp.bfloat16

LANES = 128
D_MODEL = 1024
PAST_LEN = 16384
A_HEADS = 8
A_HEAD = 64
A_WIDTH = A_HEADS * A_HEAD
A_PAIRS = A_WIDTH // LANES
LORA_W = 64
LORA_A = 64
LORA_G = 160
SHIFT_W = 3 * A_WIDTH + LORA_W + LORA_A + LORA_G
SHIFT_PAD = 1920
GN_EPS_A = 64e-5
B_HEADS = 8
B_DK = 64
B_DV = 128
B_QK = B_HEADS * B_DK
B_V = B_HEADS * B_DV
RET_W = 2 * B_QK + 2 * B_V
RET_CHUNK = 128
ROPE_BASE = 10000.0
GN_EPS_B = 1e-5
D_FF = 2816
RMS_EPS = 1e-6
WKV_CHUNK = 64

VMEM_LIMIT = 56 * 1024 * 1024


def _resident(shape):
    nd = len(shape)
    return pl.BlockSpec(shape, lambda *_: (0,) * nd, pipeline_mode=pl.Buffered(1))


def _dot(a, b):
    return jnp.dot(a, b, preferred_element_type=F32)


def _dot_nt(a, b):
    return lax.dot_general(a, b, (((1,), (1,)), ((), ())), preferred_element_type=F32)


def _dot_tn(a, b):
    return lax.dot_general(a, b, (((0,), (0,)), ((), ())), preferred_element_type=F32)


def _rms(x, g):
    return x * lax.rsqrt(jnp.mean(x * x, -1, keepdims=True) + RMS_EPS) * g


def _half_sums(x):
    rows, width = x.shape
    lo = lax.broadcasted_iota(jnp.int32, (rows, LANES), 1) < A_HEAD
    out = []
    for p in range(width // LANES):
        xp = x[:, p * LANES:(p + 1) * LANES]
        s_lo = jnp.sum(jnp.where(lo, xp, 0.0), -1, keepdims=True)
        s_hi = jnp.sum(jnp.where(lo, 0.0, xp), -1, keepdims=True)
        out.append(jnp.where(lo, s_lo, s_hi))
    return jnp.concatenate(out, -1)


def _in_proj_kernel(x_ref, g_ref, w_ref, za_ref, zb_ref, zg_ref):
    hb = _rms(x_ref[...], g_ref[...]).astype(BF16)
    za_ref[...] = _dot(hb, w_ref[:, :SHIFT_PAD])
    zb_ref[...] = _dot(hb, w_ref[:, SHIFT_PAD:SHIFT_PAD + RET_W])
    zg_ref[...] = _dot(hb, w_ref[:, SHIFT_PAD + RET_W:])


def _in_proj(x, g, w, tm):
    t = x.shape[0]
    row = lambda n: pl.BlockSpec((tm, n), lambda i: (i, 0))
    return pl.pallas_call(
        _in_proj_kernel,
        grid=(t // tm,),
        in_specs=[row(D_MODEL), _resident(g.shape), _resident(w.shape)],
        out_specs=[row(SHIFT_PAD), row(RET_W), row(2 * D_MODEL)],
        out_shape=[jax.ShapeDtypeStruct((t, SHIFT_PAD), F32), jax.ShapeDtypeStruct((t, RET_W), F32),
                   jax.ShapeDtypeStruct((t, 2 * D_MODEL), F32)],
        compiler_params=pltpu.CompilerParams(dimension_semantics=("parallel",),
                                             vmem_limit_bytes=VMEM_LIMIT),
        name="in_proj",
    )(x, g, w)


def _softplus(y):
    return jnp.maximum(y, 0.0) + jnp.log(1.0 + jnp.exp(-jnp.abs(y)))


def _prep_kernel(*refs, seq_len, seg, has_shift):
    if has_shift:
        za_ref, halo_ref, shift_ref = refs[:3]
        refs = refs[3:]
    else:
        za_ref, halo_ref = refs[:2]
        shift_ref = None
        refs = refs[2:]
    (mu_ref, wlora_ref, lg_ref, w0_ref, a0_ref, kk_ref, ka_ref, rk_ref,
     lkk_ref, lr_ref, rhk_ref, rhb_ref, uk_ref, ub_ref, v_ref, wc_ref, g_ref, bonus_ref) = refs

    za = za_ref[...]
    rows = za.shape[0]
    row = lax.broadcasted_iota(jnp.int32, (rows, 1), 0)
    prev = pltpu.roll(za, 1, axis=0)
    prev = jnp.where(row == 0, halo_ref[7:8, :], prev)
    if has_shift:
        prev = jnp.where(row % seq_len == 0, shift_ref[...], prev)
    else:
        starts = (pl.program_id(0) * rows) % seq_len == 0
        prev = jnp.where((row == 0) & starts, 0.0, prev)
    zs = za + mu_ref[...] * (prev - za)

    r = zs[:, 0:A_WIDTH]
    k = zs[:, A_WIDTH:2 * A_WIDTH]
    v = zs[:, 2 * A_WIDTH:3 * A_WIDTH]
    dwa = zs[:, 3 * A_WIDTH:3 * A_WIDTH + LANES]
    dg = zs[:, 3 * A_WIDTH + LANES:SHIFT_PAD]
    lane = lax.broadcasted_iota(jnp.int32, (rows, LANES), 1)
    act = jnp.where(lane < LORA_W, jnp.tanh(dwa), dwa).astype(BF16)
    lora = _dot(act, wlora_ref[...])
    w = -_softplus(-(w0_ref[...] + lora[:, :A_WIDTH])) - 0.5
    lw = -jnp.exp(w)
    a = jax.nn.sigmoid(a0_ref[...] + lora[:, A_WIDTH:])
    g_ref[...] = _dot(jax.nn.sigmoid(dg).astype(BF16), lg_ref[...])

    kk = k * kk_ref[...]
    kk = kk * lax.rsqrt(jnp.maximum(_half_sums(kk * kk), 1e-24))
    k2 = k * (1.0 + (a - 1.0) * ka_ref[...])
    b = kk * a
    bonus_ref[...] = _half_sums(r * k2 * rk_ref[...]) * v
    v_ref[...] = v.astype(BF16)

    ti = lax.broadcasted_iota(jnp.int32, (LANES, LANES), 0)
    tj = lax.broadcasted_iota(jnp.int32, (LANES, LANES), 1)
    same = (ti // seg) == (tj // seg)
    tri = jnp.concatenate([jnp.where(same & (tj <= ti), 1.0, 0.0),
                           jnp.where(same, 1.0, 0.0)], 0).astype(BF16)
    for g0 in range(0, rows, LANES):
        sl = slice(g0, g0 + LANES)
        x = lw[sl]
        hi = x.astype(BF16)
        r1 = x - hi.astype(F32)
        mid = r1.astype(BF16)
        lo = (r1 - mid.astype(F32)).astype(BF16)
        ct = _dot(tri, hi) + _dot(tri, mid) + _dot(tri, lo)
        cum, tot = ct[:LANES], ct[LANES:]
        inv = jnp.exp(-cum)
        tail = jnp.exp(tot - cum)
        lkk_ref[sl, :] = (kk[sl] * jnp.exp(cum - x)).astype(BF16)
        lr_ref[sl, :] = (r[sl] * jnp.exp(cum)).astype(BF16)
        rhk_ref[sl, :] = (k2[sl] * inv).astype(BF16)
        rhb_ref[sl, :] = (b[sl] * inv).astype(BF16)
        uk_ref[sl, :] = (k2[sl] * tail).astype(BF16)
        ub_ref[sl, :] = (-(b[sl] * tail)).astype(BF16)
        for s in range(LANES // seg):
            wc_ref[g0 // seg + s] = jnp.exp(tot[s * seg:s * seg + 1, :])


def _prep(za, shift_rows, pp, seq_len, seg, rows):
    t = za.shape[0]
    has_shift = shift_rows is not None
    blk = lambda n: pl.BlockSpec((rows, n), lambda i: (i, 0))
    halo = pl.BlockSpec((8, SHIFT_PAD), lambda i: (jnp.maximum(i * (rows // 8) - 1, 0), 0))
    params = [pp['mu'], pp['wlora'], pp['lg'], pp['w0'], pp['a0'], pp['k_k'], pp['k_a'], pp['r_k']]
    ins = [za, za] + ([shift_rows] if has_shift else []) + params
    in_specs = [blk(SHIFT_PAD), halo] + ([blk(SHIFT_PAD)] if has_shift else []) + \
        [_resident(p.shape) for p in params]
    bf = jax.ShapeDtypeStruct((t, A_WIDTH), BF16)
    f32 = jax.ShapeDtypeStruct((t, A_WIDTH), F32)
    out_shape = [bf] * 7 + [jax.ShapeDtypeStruct((t // seg, 1, A_WIDTH), F32), f32, f32]
    out_specs = [blk(A_WIDTH)] * 7 + [pl.BlockSpec((rows // seg, 1, A_WIDTH), lambda i: (i, 0, 0)),
                                      blk(A_WIDTH), blk(A_WIDTH)]
    return pl.pallas_call(
        functools.partial(_prep_kernel, seq_len=seq_len, seg=seg, has_shift=has_shift),
        grid=(t // rows,),
        in_specs=in_specs, out_specs=out_specs, out_shape=out_shape,
        compiler_params=pltpu.CompilerParams(dimension_semantics=("parallel",),
                                             vmem_limit_bytes=VMEM_LIMIT),
        name="rwkv_prep",
    )(*ins)


def _scan_seq_kernel(lkk_ref, lr_ref, rhk_ref, rhb_ref, uk_ref, ub_ref, v_ref, wc_ref, s0_ref,
                     g_ref, bonus_ref, gng_ref, gnb_ref, ya_ref, st_ref,
                     xlk_s, xavt_s, arb_s, yv_s, uv_s, *, nck):
    c64 = WKV_CHUNK

    @pl.when(pl.program_id(1) == 0)
    def _():
        st_ref[...] = s0_ref[...]

    ti = lax.broadcasted_iota(jnp.int32, (LANES, LANES), 0)
    tj = lax.broadcasted_iota(jnp.int32, (LANES, LANES), 1)
    same_head = (ti // c64) == (tj // c64)
    strict = same_head & (tj < ti)
    incl = same_head & (tj <= ti)
    eye = jnp.where(ti == tj, 1.0, 0.0)
    level = lambda s: ((ti // s) % 2 == 1) & ((tj // s) == (ti // s) - 1)
    levels = [2 ** e for e in range(1, 6)]
    level_b = {s: jnp.where(level(s), 1.0, 0.0).astype(BF16) for s in levels}
    lane_row = lax.broadcasted_iota(jnp.int32, (1, LANES), 1)
    m_lo = jnp.where(lane_row < A_HEAD, 1.0, 0.0).astype(BF16)
    m_hi = jnp.where(lane_row < A_HEAD, 0.0, 1.0).astype(BF16)
    lo_half = lax.broadcasted_iota(jnp.int32, (c64, LANES), 1) < A_HEAD
    stack_heads = lambda x: jnp.concatenate([x * m_lo, x * m_hi], 0)
    twice = lambda x: jnp.concatenate([x, x], 0)
    unstack = lambda x: jnp.where(lo_half, x[:c64], x[c64:])
    bf = lambda x: x.astype(BF16)

    units = [(c, p) for c in range(nck) for p in range(A_PAIRS)]
    group = 8
    for g0 in range(0, len(units), group):
        grp = [g0 + i for i in range(len(units[g0:g0 + group]))]
        sl = {u: (slice(units[u][0] * c64, (units[u][0] + 1) * c64),
                  slice(units[u][1] * LANES, (units[u][1] + 1) * LANES)) for u in grp}
        akk, akb, ark, xs = {}, {}, {}, {}
        for u in grp:
            rs, cs = sl[u]
            a4 = _dot_nt(jnp.concatenate([stack_heads(lkk_ref[rs, cs]), stack_heads(lr_ref[rs, cs])], 0),
                         jnp.concatenate([twice(rhk_ref[rs, cs]), twice(rhb_ref[rs, cs])], 0))
            akk[u] = bf(jnp.where(strict, a4[:LANES, :LANES], 0.0))
            a_kb = jnp.where(strict, a4[:LANES, LANES:], 0.0)
            ark[u] = bf(jnp.where(incl, a4[LANES:, :LANES], 0.0))
            arb_s[u] = bf(jnp.where(incl, a4[LANES:, LANES:], 0.0))
            akb[u] = bf(a_kb)
            xs[u] = eye - jnp.where(level(1), a_kb, 0.0)
        for s in levels:
            xb = {u: bf(xs[u]) for u in grp}
            t1 = {u: bf(_dot(xb[u], akb[u] * level_b[s])) for u in grp}
            xs = {u: xs[u] - _dot(t1[u], xb[u]) for u in grp}
        xb = {u: bf(xs[u]) for u in grp}
        vv = {u: twice(v_ref[sl[u]]) for u in grp}
        for u in grp:
            xlk_s[u] = bf(_dot(xb[u], twice(lkk_ref[sl[u]])))
        av = {u: bf(_dot(akk[u], vv[u])) for u in grp}
        for u in grp:
            xavt_s[u] = _dot(xb[u], av[u]).T
        for u in grp:
            yv_s[u] = unstack(_dot(ark[u], vv[u]))
        for u in grp:
            uv_s[u] = jnp.where(same_head, _dot_tn(v_ref[sl[u]], uk_ref[sl[u]]), 0.0)

    pairs = range(A_PAIRS)
    cols = [slice(p * LANES, (p + 1) * LANES) for p in pairs]
    for c in range(nck):
        rs = slice(c * c64, (c + 1) * c64)
        us = [c * A_PAIRS + p for p in pairs]
        st = [st_ref[0, p] for p in pairs]
        stb = [bf(s_) for s_ in st]
        skt = [bf(jnp.where(same_head, _dot_nt(stb[p], xlk_s[us[p]]) + xavt_s[us[p]], 0.0))
               for p in pairs]
        upd = [_dot(skt[p], stack_heads(ub_ref[rs, cols[p]])) for p in pairs]
        for p in pairs:
            st_ref[0, p] = st[p] * wc_ref[c, :, cols[p]] + uv_s[us[p]] + upd[p]
        g_r = [_dot_nt(lr_ref[rs, cols[p]], stb[p]) for p in pairs]
        y2 = [_dot_nt(arb_s[us[p]], skt[p]) for p in pairs]
        y = jnp.concatenate([g_r[p] + yv_s[us[p]] - (y2[p][:c64] + y2[p][c64:]) for p in pairs], -1)
        mu = _half_sums(y) * (1.0 / A_HEAD)
        d = y - mu
        var = _half_sums(d * d) * (1.0 / A_HEAD)
        yn = d * lax.rsqrt(var + GN_EPS_A) * gng_ref[...] + gnb_ref[...]
        ya_ref[rs, :] = ((yn + bonus_ref[rs, :]) * g_ref[rs, :]).astype(BF16)


def _scan_multi_kernel(lkk_ref, lr_ref, rhk_ref, rhb_ref, uk_ref, ub_ref, v_ref, wc_ref, s0_ref,
                       g_ref, bonus_ref, gng_ref, gnb_ref, ya_ref, st_ref, *, nck, nseg):
    c64 = WKV_CHUNK
    seg = c64 // nseg
    st_ref[...] = s0_ref[...]

    ti = lax.broadcasted_iota(jnp.int32, (LANES, LANES), 0)
    tj = lax.broadcasted_iota(jnp.int32, (LANES, LANES), 1)
    same_head = (ti // c64) == (tj // c64)
    same_seg = (ti // seg) == (tj // seg)
    strict = same_seg & (tj < ti)
    incl = same_seg & (tj <= ti)
    eye = jnp.where(ti == tj, 1.0, 0.0)
    lane_row = lax.broadcasted_iota(jnp.int32, (1, LANES), 1)
    m_lo = jnp.where(lane_row < A_HEAD, 1.0, 0.0).astype(BF16)
    m_hi = jnp.where(lane_row < A_HEAD, 0.0, 1.0).astype(BF16)
    lo_half = lax.broadcasted_iota(jnp.int32, (c64, LANES), 1) < A_HEAD
    stack_heads = lambda x: jnp.concatenate([x * m_lo, x * m_hi], 0)
    twice = lambda x: jnp.concatenate([x, x], 0)
    unstack = lambda x: jnp.where(lo_half, x[:c64], x[c64:])

    bf = lambda x: x.astype(BF16)
    level = lambda s: ((ti // s) % 2 == 1) & ((tj // s) == (ti // s) - 1)
    segs = [slice(s_ * seg, (s_ + 1) * seg) for s_ in range(nseg)]

    units = [(c, p) for c in range(nck) for p in range(A_PAIRS)]
    sl = {u: (slice(u[0] * c64, (u[0] + 1) * c64), slice(u[1] * LANES, (u[1] + 1) * LANES)) for u in units}
    akk, akb, ark, arb, xs = {}, {}, {}, {}, {}
    for u in units:
        rs, cs = sl[u]
        a4 = _dot_nt(jnp.concatenate([stack_heads(lkk_ref[rs, cs]), stack_heads(lr_ref[rs, cs])], 0),
                     jnp.concatenate([twice(rhk_ref[rs, cs]), twice(rhb_ref[rs, cs])], 0))
        akk[u] = bf(jnp.where(strict, a4[:LANES, :LANES], 0.0))
        a_kb = jnp.where(strict, a4[:LANES, LANES:], 0.0)
        ark[u] = bf(jnp.where(incl, a4[LANES:, :LANES], 0.0))
        arb[u] = bf(jnp.where(incl, a4[LANES:, LANES:], 0.0))
        akb[u] = bf(a_kb)
        xs[u] = eye - jnp.where(level(1), a_kb, 0.0)
    s = 2
    while s < seg:
        lvl_b = jnp.where(level(s), 1.0, 0.0).astype(BF16)
        xb = {u: bf(xs[u]) for u in units}
        t1 = {u: bf(_dot(xb[u], akb[u] * lvl_b)) for u in units}
        xs = {u: xs[u] - _dot(t1[u], xb[u]) for u in units}
        s *= 2
    xb = {u: bf(xs[u]) for u in units}
    vv = {u: twice(v_ref[sl[u]]) for u in units}

    lkf = {u: lkk_ref[sl[u]].astype(F32) for u in units}
    lrf = {u: lr_ref[sl[u]].astype(F32) for u in units}
    g_k = {(c, p): jnp.concatenate([_dot_nt(lkf[c, p][ss], st_ref[c * nseg + i, p])
                                    for i, ss in enumerate(segs)], 0) for c, p in units}
    g_r = {(c, p): jnp.concatenate([_dot_nt(lrf[c, p][ss], st_ref[c * nseg + i, p])
                                    for i, ss in enumerate(segs)], 0) for c, p in units}
    rhs = {u: bf(twice(g_k[u]) + _dot(akk[u], vv[u])) for u in units}
    sk = {u: unstack(_dot(xb[u], rhs[u])) for u in units}
    y2 = {u: _dot(ark[u], vv[u]) - _dot(arb[u], twice(bf(sk[u]))) for u in units}
    ys = {u: g_r[u] + unstack(y2[u]) for u in units}
    for c, p in units:
        rs, cs = sl[c, p]
        vf, ukf, ubf = v_ref[rs, cs].astype(F32), uk_ref[rs, cs].astype(F32), ub_ref[rs, cs].astype(F32)
        for i, ss in enumerate(segs):
            upd = _dot_tn(jnp.concatenate([vf[ss], sk[c, p][ss]], 0),
                          jnp.concatenate([ukf[ss], ubf[ss]], 0))
            n = c * nseg + i
            st_ref[n, p] = st_ref[n, p] * wc_ref[n, :, cs] + jnp.where(same_head, upd, 0.0)

    for c in range(nck):
        rs = slice(c * c64, (c + 1) * c64)
        y = jnp.concatenate([ys[c, p] for p in range(A_PAIRS)], -1)
        mu = _half_sums(y) * (1.0 / A_HEAD)
        d = y - mu
        var = _half_sums(d * d) * (1.0 / A_HEAD)
        yn = d * lax.rsqrt(var + GN_EPS_A) * gng_ref[...] + gnb_ref[...]
        ya_ref[rs, :] = ((yn + bonus_ref[rs, :]) * g_ref[rs, :]).astype(BF16)


def _scan(pre, s0, gn_g, gn_b, nb, nj, nck, nseg, carry):
    lkk, lr, rhk, rhb, uk, ub, v, wc, g, bonus = pre
    t = lkk.shape[0]
    rows = nck * WKV_CHUNK
    nst = s0.shape[0] // nb
    blk = pl.BlockSpec((rows, A_WIDTH), lambda i, j: (i * nj + j, 0))
    wcs = pl.BlockSpec((nck * nseg, 1, A_WIDTH), lambda i, j: (i * nj + j, 0, 0))
    sts = pl.BlockSpec((nst, A_PAIRS, LANES, LANES), lambda i, j: (i, 0, 0, 0))
    vec = pl.BlockSpec((1, A_WIDTH), lambda i, j: (0, 0))
    if carry:
        units = nck * A_PAIRS
        body = functools.partial(_scan_seq_kernel, nck=nck)
        scratch = [pltpu.VMEM((units, LANES, LANES), BF16), pltpu.VMEM((units, LANES, LANES), F32),
                   pltpu.VMEM((units, LANES, LANES), BF16), pltpu.VMEM((units, WKV_CHUNK, LANES), F32),
                   pltpu.VMEM((units, LANES, LANES), F32)]
    else:
        body = functools.partial(_scan_multi_kernel, nck=nck, nseg=nseg)
        scratch = []
    return pl.pallas_call(
        body,
        grid=(nb, nj),
        in_specs=[blk] * 7 + [wcs, sts, blk, blk, vec, vec],
        out_specs=[blk, sts],
        scratch_shapes=scratch,
        out_shape=[jax.ShapeDtypeStruct((t, A_WIDTH), BF16), jax.ShapeDtypeStruct(s0.shape, F32)],
        compiler_params=pltpu.CompilerParams(dimension_semantics=("parallel", "arbitrary"),
                                             vmem_limit_bytes=VMEM_LIMIT),
        name="rwkv_scan",
    )(lkk, lr, rhk, rhb, uk, ub, v, wc, s0, g, bonus, gn_g, gn_b)


def _ret_kernel(q_ref, k_ref, v_ref, gate_ref, cos_ref, sin_ref, din_ref, dq_ref, dk_ref, dc_ref,
                s0_ref, gng_ref, gnb_ref, yb_ref, st_ref, *, nck, nseg, carry):
    ch = RET_CHUNK
    seg = ch // nseg
    if carry:
        @pl.when(pl.program_id(1) == 0)
        def _():
            st_ref[...] = s0_ref[...]
    else:
        st_ref[...] = s0_ref[...]

    lane = lax.broadcasted_iota(jnp.int32, (ch, B_QK), 1)
    first = (lane % B_DK) < (B_DK // 2)
    lane_row = lax.broadcasted_iota(jnp.int32, (1, LANES), 1)
    masks = [jnp.where(lane_row < B_DK, 1.0, 0.0), jnp.where(lane_row < B_DK, 0.0, 1.0)]

    def rotary(x, cos, sin):
        swapped = jnp.where(first, pltpu.roll(x, B_QK - B_DK // 2, axis=1),
                            pltpu.roll(x, B_DK // 2, axis=1))
        return x * cos + swapped * sin

    for c in range(nck):
        rs = slice(c * ch, (c + 1) * ch)
        cos = jnp.concatenate([cos_ref[rs, :]] * (B_QK // LANES), -1)
        sin = jnp.concatenate([sin_ref[rs, :]] * (B_QK // LANES), -1)
        q = rotary(q_ref[rs, :], cos, sin)
        k = rotary(k_ref[rs, :], cos, sin) * (B_DK ** -0.5)
        for h in range(B_HEADS):
            ps = slice((h // 2) * LANES, (h // 2 + 1) * LANES)
            hs = slice(h * B_DV, (h + 1) * B_DV)
            qp = q[:, ps]
            km = k[:, ps] * masks[h % 2]
            vb = v_ref[rs, hs].astype(BF16)
            scores = _dot_nt(qp.astype(BF16), km.astype(BF16)) * din_ref[h]
            o = _dot(scores.astype(BF16), vb)
            qd = qp * dq_ref[h]
            kd = km * dk_ref[h]
            dc = dc_ref[h][0:1, :]
            own = slice((h % 2) * B_DK, (h % 2 + 1) * B_DK)
            zero = jnp.zeros((B_DK, B_DV), F32)
            pad = lambda s_: jnp.concatenate([s_, zero] if h % 2 == 0 else [zero, s_], 0)
            if nseg == 1:
                st = st_ref[0, h]
                o = o + _dot(qd.astype(BF16), pad(st).astype(BF16))
                st_ref[0, h] = st * dc + _dot_tn(kd.astype(BF16), vb)[own]
            else:
                vf = v_ref[rs, hs]
                cross = []
                for s_ in range(nseg):
                    ss = slice(s_ * seg, (s_ + 1) * seg)
                    i = c * nseg + s_
                    st = st_ref[i, h]
                    cross.append(_dot(qd[ss], pad(st)))
                    st_ref[i, h] = st * dc + _dot_tn(kd[ss], vf[ss])[own]
                o = o + jnp.concatenate(cross, 0)
            mu = jnp.mean(o, -1, keepdims=True)
            d = o - mu
            var = jnp.mean(d * d, -1, keepdims=True)
            yn = d * lax.rsqrt(var + GN_EPS_B) * gng_ref[:, hs] + gnb_ref[:, hs]
            gate = gate_ref[rs, hs]
            yb_ref[rs, hs] = (yn * (gate * jax.nn.sigmoid(gate))).astype(BF16)


def _retention(zb, tabs, s0, gn_g, gn_b, nb, nj, nck, nseg, carry, tab_rows_fixed):
    cos, sin, din, dq, dk, dc = tabs
    t = zb.shape[0]
    rows = nck * RET_CHUNK
    nst = s0.shape[0] // nb
    qs = pl.BlockSpec((rows, B_QK), lambda i, j: (i * nj + j, 0))
    ks = pl.BlockSpec((rows, B_QK), lambda i, j: (i * nj + j, 1))
    vs = pl.BlockSpec((rows, B_V), lambda i, j: (i * nj + j, 1))
    gs = pl.BlockSpec((rows, B_V), lambda i, j: (i * nj + j, 2))
    if tab_rows_fixed:
        tab = pl.BlockSpec((rows, LANES), lambda i, j: (0, 0))
    else:
        tab = pl.BlockSpec((rows, LANES), lambda i, j: (j, 0))
    sts = pl.BlockSpec((nst, B_HEADS, B_DK, B_DV), lambda i, j: (i, 0, 0, 0))
    const = lambda a: pl.BlockSpec(a.shape, lambda i, j: (0,) * a.ndim)
    return pl.pallas_call(
        functools.partial(_ret_kernel, nck=nck, nseg=nseg, carry=carry),
        grid=(nb, nj),
        in_specs=[qs, ks, vs, gs, tab, tab, const(din), const(dq), const(dk), const(dc), sts,
                  const(gn_g), const(gn_b)],
        out_specs=[pl.BlockSpec((rows, B_V), lambda i, j: (i * nj + j, 0)), sts],
        out_shape=[jax.ShapeDtypeStruct((t, B_V), BF16), jax.ShapeDtypeStruct(s0.shape, F32)],
        compiler_params=pltpu.CompilerParams(dimension_semantics=("parallel", "arbitrary"),
                                             vmem_limit_bytes=VMEM_LIMIT),
        name="retention",
    )(zb, zb, zb, zb, cos, sin, din, dq, dk, dc, s0, gn_g, gn_b)


def _merge_ffn_kernel(x_ref, ya_ref, yb_ref, zg_ref, woa_ref, wob_ref, wo_ref, gpost_ref, gpre_ref,
                      wgate_ref, wup_ref, wdown_ref, gfpost_ref, out_ref):
    zg = zg_ref[...]
    m = (jax.nn.sigmoid(zg[:, :D_MODEL]) * _dot(ya_ref[...], woa_ref[...])
         + jax.nn.sigmoid(zg[:, D_MODEL:]) * _dot(yb_ref[...], wob_ref[...]))
    x1 = x_ref[...] + _rms(_dot(m.astype(BF16), wo_ref[...]), gpost_ref[...])
    hb = _rms(x1, gpre_ref[...]).astype(BF16)
    gate = _dot(hb, wgate_ref[...])
    up = _dot(hb, wup_ref[...])
    f = _dot((gate * jax.nn.sigmoid(gate) * up).astype(BF16), wdown_ref[...])
    out_ref[...] = x1 + _rms(f, gfpost_ref[...])


def _merge_ffn(x, ya, yb, zg, mp, tm):
    t = x.shape[0]
    row = lambda n: pl.BlockSpec((tm, n), lambda i: (i, 0))
    ws = [mp['w_out_a'], mp['w_out_b'], mp['w_o'], mp['g_post'], mp['g_ffn_pre'], mp['w_gate'],
          mp['w_up'], mp['w_down'], mp['g_ffn_post']]
    return pl.pallas_call(
        _merge_ffn_kernel,
        grid=(t // tm,),
        in_specs=[row(D_MODEL), row(A_WIDTH), row(B_V), row(2 * D_MODEL)] +
                 [_resident(w.shape) for w in ws],
        out_specs=row(D_MODEL),
        out_shape=jax.ShapeDtypeStruct((t, D_MODEL), F32),
        compiler_params=pltpu.CompilerParams(dimension_semantics=("parallel",),
                                             vmem_limit_bytes=VMEM_LIMIT),
        name="merge_ffn",
    )(x, ya, yb, zg, *ws)


def _layer_params(l, w_in, mu_shift, w0, lora_w_up, a0, lora_a_up, lora_g_up, k_k, k_a, r_k):
    wi = w_in[l]
    row = lambda a: a.reshape(1, -1)
    wlora = jnp.zeros((LANES, 2 * A_WIDTH), F32)
    wlora = wlora.at[:LORA_W, :A_WIDTH].set(lora_w_up[l]).at[LORA_W:, A_WIDTH:].set(lora_a_up[l])
    lg = jnp.zeros((SHIFT_PAD - 3 * A_WIDTH - LANES, A_WIDTH), F32).at[:LORA_G].set(lora_g_up[l])
    return {
        'w_in': jnp.concatenate([wi[:, :SHIFT_W].astype(BF16),
                                 jnp.zeros((D_MODEL, SHIFT_PAD - SHIFT_W), BF16),
                                 wi[:, SHIFT_W:].astype(BF16)], 1),
        'mu': jnp.pad(row(mu_shift[l]), ((0, 0), (0, SHIFT_PAD - SHIFT_W))),
        'wlora': wlora.astype(BF16), 'lg': lg.astype(BF16),
        'w0': row(w0[l]), 'a0': row(a0[l]), 'k_k': row(k_k[l]), 'k_a': row(k_a[l]), 'r_k': row(r_k[l]),
    }


def _ret_tables(pos, seg, rows):
    half = B_DK // 2
    inv = ROPE_BASE ** (-jnp.arange(half, dtype=F32) / half)
    ang = pos.astype(F32)[:, None] * inv[None, :]
    cos, sin = jnp.cos(ang), jnp.sin(ang)
    cos = jnp.tile(jnp.concatenate([cos, cos], -1), (1, LANES // B_DK))
    sin = jnp.tile(jnp.concatenate([-sin, sin], -1), (1, LANES // B_DK))
    log_g = jnp.log(1.0 - jnp.exp2(-5.0 - jnp.arange(B_HEADS, dtype=F32)))
    idx = jnp.arange(rows)
    loc = (idx % seg).astype(F32)
    rel = loc[:, None] - loc[None, :]
    ok = ((idx[:, None] // seg) == (idx[None, :] // seg)) & (rel >= 0)
    din = jnp.where(ok[None], jnp.exp(log_g[:, None, None] * jnp.where(ok, rel, 0.0)[None]), 0.0)
    bc = lambda col: jnp.broadcast_to(col[:, :, None], (B_HEADS, rows, LANES))
    dq = bc(jnp.exp(log_g[:, None] * (loc[None, :] + 1.0)))
    dk = bc(jnp.exp(log_g[:, None] * (seg - 1.0 - loc[None, :])))
    dc = jnp.broadcast_to(jnp.exp(log_g * seg)[:, None, None], (B_HEADS, 8, LANES))
    return cos, sin, din, dq, dk, dc


def _expand_wkv(s):
    b = s.shape[0]
    s = s.reshape(b, A_PAIRS, 2, A_HEAD, A_HEAD)
    z = jnp.zeros((b, A_PAIRS, A_HEAD, A_HEAD), s.dtype)
    return jnp.concatenate([jnp.concatenate([s[:, :, 0], z], -1),
                            jnp.concatenate([z, s[:, :, 1]], -1)], -2)


def _extract_wkv(s):
    b = s.shape[0]
    return jnp.stack([s[:, :, :A_HEAD, :A_HEAD], s[:, :, A_HEAD:, A_HEAD:]], 2).reshape(
        b, A_HEADS, A_HEAD, A_HEAD)


def _group_layer(x, nseq, seq_len, pos0, states, lp, mp, gn):
    t = x.shape[0]
    long_seq = seq_len >= RET_CHUNK
    tm = 256
    za, zb, zg = _in_proj(x, lp['g_pre'], lp['w_in'], tm)

    if long_seq:
        wkv_nseg, ret_nseg = 1, 1
        nck_a, nck_b = 4, 2
        prep_rows = 512
        nb, nj_a, nj_b = nseq, seq_len // (nck_a * WKV_CHUNK), seq_len // (nck_b * RET_CHUNK)
        pos = pos0 + jnp.arange(seq_len)
        tabs = _ret_tables(pos, RET_CHUNK, RET_CHUNK)
    else:
        wkv_nseg, ret_nseg = WKV_CHUNK // seq_len, RET_CHUNK // seq_len
        nck_a, nck_b = 2, 1
        prep_rows = RET_CHUNK
        nb, nj_a, nj_b = t // (nck_a * WKV_CHUNK), 1, 1
        pos = pos0 + (jnp.arange(RET_CHUNK) % seq_len)
        tabs = _ret_tables(pos, seq_len, RET_CHUNK)
    nb_b = t // (nck_b * RET_CHUNK * nj_b)

    if states is None:
        s_wkv = jnp.zeros((nseq, A_PAIRS, LANES, LANES), F32)
        s_ret = jnp.zeros((nseq, B_HEADS, B_DK, B_DV), F32)
        shift_rows = None
    else:
        s_wkv = _expand_wkv(states[0])
        s_ret = states[1]
        shift_rows = jnp.repeat(jnp.pad(states[2], ((0, 0), (0, SHIFT_PAD - SHIFT_W))), seq_len, axis=0)

    pre = _prep(za, shift_rows, lp, seq_len, min(seq_len, WKV_CHUNK), prep_rows)
    ya, wkv = _scan(pre, s_wkv, gn['a_g'], gn['a_b'], nb, nj_a, nck_a, wkv_nseg, long_seq)
    yb, ret = _retention(zb, tabs, s_ret, gn['b_g'], gn['b_b'], nb_b, nj_b, nck_b, ret_nseg, long_seq,
                         not long_seq)
    x = _merge_ffn(x, ya, yb, zg, mp, tm)
    shift = za.reshape(nseq, seq_len, SHIFT_PAD)[:, -1, :SHIFT_W]
    return x, _extract_wkv(wkv), ret, shift


def kernel(x_prompt, x_sample, state_wkv, state_ret, state_shift, norm_mix_pre, w_in, mu_shift, w0,
           lora_w_up, a0, lora_a_up, lora_g_up, k_k, k_a, r_k, gn_a_gain, gn_a_bias, w_out_a,
           gn_b_gain, gn_b_bias, w_out_b, w_o, norm_mix_post, norm_ffn_pre, w_ffn_gate, w_ffn_up,
           w_ffn_down, norm_ffn_post):
    bp, lp_, _ = x_prompt.shape
    bs, ls_, _ = x_sample.shape
    depth = w_in.shape[0]
    yp = x_prompt.reshape(bp * lp_, D_MODEL)
    ys = x_sample.reshape(bs * ls_, D_MODEL)
    row = lambda a: a.reshape(1, -1)
    outs = [[] for _ in range(6)]
    for l in range(depth):
        lp = _layer_params(l, w_in, mu_shift, w0, lora_w_up, a0, lora_a_up, lora_g_up, k_k, k_a, r_k)
        lp['g_pre'] = row(norm_mix_pre[l])
        mp = {'w_out_a': w_out_a[l].astype(BF16), 'w_out_b': w_out_b[l].astype(BF16),
              'w_o': w_o[l].astype(BF16), 'g_post': row(norm_mix_post[l]),
              'g_ffn_pre': row(norm_ffn_pre[l]), 'w_gate': w_ffn_gate[l].astype(BF16),
              'w_up': w_ffn_up[l].astype(BF16), 'w_down': w_ffn_down[l].astype(BF16),
              'g_ffn_post': row(norm_ffn_post[l])}
        gn = {'a_g': row(gn_a_gain[l]), 'a_b': row(gn_a_bias[l]),
              'b_g': row(gn_b_gain[l]), 'b_b': row(gn_b_bias[l])}
        yp, a_, b_, c_ = _group_layer(yp, bp, lp_, 0, None, lp, mp, gn)
        outs[0].append(a_); outs[1].append(b_); outs[2].append(c_)
        ys, a_, b_, c_ = _group_layer(ys, bs, ls_, PAST_LEN,
                                      (state_wkv[l], state_ret[l], state_shift[l]), lp, mp, gn)
        outs[3].append(a_); outs[4].append(b_); outs[5].append(c_)
    return (yp.reshape(bp, lp_, D_MODEL), ys.reshape(bs, ls_, D_MODEL)) + \
        tuple(jnp.stack(o) for o in outs)
```

```python
import functools

import jax
import jax.numpy as jnp
from jax import lax
from jax.experimental import pallas as pl
from jax.experimental.pallas import tpu as pltpu

F32 = jnp.float32
BF16 = jnp.bfloat16

LANES = 128
D_MODEL = 1024
PAST_LEN = 16384
A_HEADS = 8
A_HEAD = 64
A_WIDTH = A_HEADS * A_HEAD
A_PAIRS = A_WIDTH // LANES
LORA_W = 64
LORA_A = 64
LORA_G = 160
SHIFT_W = 3 * A_WIDTH + LORA_W + LORA_A + LORA_G
SHIFT_PAD = 1920
GN_EPS_A = 64e-5
B_HEADS = 8
B_DK = 64
B_DV = 128
B_QK = B_HEADS * B_DK
B_V = B_HEADS * B_DV
RET_W = 2 * B_QK + 2 * B_V
RET_CHUNK = 128
ROPE_BASE = 10000.0
GN_EPS_B = 1e-5
D_FF = 2816
RMS_EPS = 1e-6
WKV_CHUNK = 64

VMEM_LIMIT = 56 * 1024 * 1024


def _resident(shape):
    nd = len(shape)
    return pl.BlockSpec(shape, lambda *_: (0,) * nd, pipeline_mode=pl.Buffered(1))


def _dot(a, b):
    return jnp.dot(a, b, preferred_element_type=F32)


def _dot_nt(a, b):
    return lax.dot_general(a, b, (((1,), (1,)), ((), ())), preferred_element_type=F32)


def _dot_tn(a, b):
    return lax.dot_general(a, b, (((0,), (0,)), ((), ())), preferred_element_type=F32)


def _rms(x, g):
    return x * lax.rsqrt(jnp.mean(x * x, -1, keepdims=True) + RMS_EPS) * g


def _half_sums(x):
    rows, width = x.shape
    lo = lax.broadcasted_iota(jnp.int32, (rows, LANES), 1) < A_HEAD
    out = []
    for p in range(width // LANES):
        xp = x[:, p * LANES:(p + 1) * LANES]
        s_lo = jnp.sum(jnp.where(lo, xp, 0.0), -1, keepdims=True)
        s_hi = jnp.sum(jnp.where(lo, 0.0, xp), -1, keepdims=True)
        out.append(jnp.where(lo, s_lo, s_hi))
    return jnp.concatenate(out, -1)


def _in_proj_kernel(x_ref, g_ref, w_ref, za_ref, zb_ref, zg_ref):
    hb = _rms(x_ref[...], g_ref[...]).astype(BF16)
    za_ref[...] = _dot(hb, w_ref[:, :SHIFT_PAD])
    zb_ref[...] = _dot(hb, w_ref[:, SHIFT_PAD:SHIFT_PAD + RET_W])
    zg_ref[...] = _dot(hb, w_ref[:, SHIFT_PAD + RET_W:])


def _in_proj(x, g, w, tm):
    t = x.shape[0]
    row = lambda n: pl.BlockSpec((tm, n), lambda i: (i, 0))
    return pl.pallas_call(
        _in_proj_kernel,
        grid=(t // tm,),
        in_specs=[row(D_MODEL), _resident(g.shape), _resident(w.shape)],
        out_specs=[row(SHIFT_PAD), row(RET_W), row(2 * D_MODEL)],
        out_shape=[jax.ShapeDtypeStruct((t, SHIFT_PAD), F32), jax.ShapeDtypeStruct((t, RET_W), F32),
                   jax.ShapeDtypeStruct((t, 2 * D_MODEL), F32)],
        compiler_params=pltpu.CompilerParams(dimension_semantics=("parallel",),
                                             vmem_limit_bytes=VMEM_LIMIT),
        name="in_proj",
    )(x, g, w)


def _softplus(y):
    return jnp.maximum(y, 0.0) + jnp.log(1.0 + jnp.exp(-jnp.abs(y)))


def _prep_kernel(*refs, seq_len, seg, has_shift):
    if has_shift:
        za_ref, halo_ref, shift_ref = refs[:3]
        refs = refs[3:]
    else:
        za_ref, halo_ref = refs[:2]
        shift_ref = None
        refs = refs[2:]
    (mu_ref, wlora_ref, lg_ref, w0_ref, a0_ref, kk_ref, ka_ref, rk_ref,
     lkk_ref, lr_ref, rhk_ref, rhb_ref, uk_ref, ub_ref, v_ref, wc_ref, g_ref, bonus_ref) = refs

    za = za_ref[...]
    rows = za.shape[0]
    row = lax.broadcasted_iota(jnp.int32, (rows, 1), 0)
    prev = pltpu.roll(za, 1, axis=0)
    prev = jnp.where(row == 0, halo_ref[7:8, :], prev)
    if has_shift:
        prev = jnp.where(row % seq_len == 0, shift_ref[...], prev)
    else:
        starts = (pl.program_id(0) * rows) % seq_len == 0
        prev = jnp.where((row == 0) & starts, 0.0, prev)
    zs = za + mu_ref[...] * (prev - za)

    r = zs[:, 0:A_WIDTH]
    k = zs[:, A_WIDTH:2 * A_WIDTH]
    v = zs[:, 2 * A_WIDTH:3 * A_WIDTH]
    dwa = zs[:, 3 * A_WIDTH:3 * A_WIDTH + LANES]
    dg = zs[:, 3 * A_WIDTH + LANES:SHIFT_PAD]
    lane = lax.broadcasted_iota(jnp.int32, (rows, LANES), 1)
    act = jnp.where(lane < LORA_W, jnp.tanh(dwa), dwa).astype(BF16)
    lora = _dot(act, wlora_ref[...])
    w = -_softplus(-(w0_ref[...] + lora[:, :A_WIDTH])) - 0.5
    lw = -jnp.exp(w)
    a = jax.nn.sigmoid(a0_ref[...] + lora[:, A_WIDTH:])
    g_ref[...] = _dot(jax.nn.sigmoid(dg).astype(BF16), lg_ref[...])

    kk = k * kk_ref[...]
    kk = kk * lax.rsqrt(jnp.maximum(_half_sums(kk * kk), 1e-24))
    k2 = k * (1.0 + (a - 1.0) * ka_ref[...])
    b = kk * a
    bonus_ref[...] = _half_sums(r * k2 * rk_ref[...]) * v
    v_ref[...] = v.astype(BF16)

    ti = lax.broadcasted_iota(jnp.int32, (LANES, LANES), 0)
    tj = lax.broadcasted_iota(jnp.int32, (LANES, LANES), 1)
    same = (ti // seg) == (tj // seg)
    tri = jnp.concatenate([jnp.where(same & (tj <= ti), 1.0, 0.0),
                           jnp.where(same, 1.0, 0.0)], 0).astype(BF16)
    for g0 in range(0, rows, LANES):
        sl = slice(g0, g0 + LANES)
        x = lw[sl]
        hi = x.astype(BF16)
        r1 = x - hi.astype(F32)
        mid = r1.astype(BF16)
        lo = (r1 - mid.astype(F32)).astype(BF16)
        ct = _dot(tri, hi) + _dot(tri, mid) + _dot(tri, lo)
        cum, tot = ct[:LANES], ct[LANES:]
        inv = jnp.exp(-cum)
        tail = jnp.exp(tot - cum)
        lkk_ref[sl, :] = (kk[sl] * jnp.exp(cum - x)).astype(BF16)
        lr_ref[sl, :] = (r[sl] * jnp.exp(cum)).astype(BF16)
        rhk_ref[sl, :] = (k2[sl] * inv).astype(BF16)
        rhb_ref[sl, :] = (b[sl] * inv).astype(BF16)
        uk_ref[sl, :] = (k2[sl] * tail).astype(BF16)
        ub_ref[sl, :] = (-(b[sl] * tail)).astype(BF16)
        for s in range(LANES // seg):
            wc_ref[g0 // seg + s] = jnp.exp(tot[s * seg:s * seg + 1, :])


def _prep(za, shift_rows, pp, seq_len, seg, rows):
    t = za.shape[0]
    has_shift = shift_rows is not None
    blk = lambda n: pl.BlockSpec((rows, n), lambda i: (i, 0))
    halo = pl.BlockSpec((8, SHIFT_PAD), lambda i: (jnp.maximum(i * (rows // 8) - 1, 0), 0))
    params = [pp['mu'], pp['wlora'], pp['lg'], pp['w0'], pp['a0'], pp['k_k'], pp['k_a'], pp['r_k']]
    ins = [za, za] + ([shift_rows] if has_shift else []) + params
    in_specs = [blk(SHIFT_PAD), halo] + ([blk(SHIFT_PAD)] if has_shift else []) + \
        [_resident(p.shape) for p in params]
    bf = jax.ShapeDtypeStruct((t, A_WIDTH), BF16)
    f32 = jax.ShapeDtypeStruct((t, A_WIDTH), F32)
    out_shape = [bf] * 7 + [jax.ShapeDtypeStruct((t // seg, 1, A_WIDTH), F32), f32, f32]
    out_specs = [blk(A_WIDTH)] * 7 + [pl.BlockSpec((rows // seg, 1, A_WIDTH), lambda i: (i, 0, 0)),
                                      blk(A_WIDTH), blk(A_WIDTH)]
    return pl.pallas_call(
        functools.partial(_prep_kernel, seq_len=seq_len, seg=seg, has_shift=has_shift),
        grid=(t // rows,),
        in_specs=in_specs, out_specs=out_specs, out_shape=out_shape,
        compiler_params=pltpu.CompilerParams(dimension_semantics=("parallel",),
                                             vmem_limit_bytes=VMEM_LIMIT),
        name="rwkv_prep",
    )(*ins)


def _scan_seq_kernel(lkk_ref, lr_ref, rhk_ref, rhb_ref, uk_ref, ub_ref, v_ref, wc_ref, s0_ref,
                     g_ref, bonus_ref, gng_ref, gnb_ref, ya_ref, st_ref,
                     xlk_s, xavt_s, arb_s, yv_s, uv_s, *, nck):
    c64 = WKV_CHUNK

    @pl.when(pl.program_id(1) == 0)
    def _():
        st_ref[...] = s0_ref[...]

    ti = lax.broadcasted_iota(jnp.int32, (LANES, LANES), 0)
    tj = lax.broadcasted_iota(jnp.int32, (LANES, LANES), 1)
    same_head = (ti // c64) == (tj // c64)
    strict = same_head & (tj < ti)
    incl = same_head & (tj <= ti)
    eye = jnp.where(ti == tj, 1.0, 0.0)
    level = lambda s: ((ti // s) % 2 == 1) & ((tj // s) == (ti // s) - 1)
    levels = [2 ** e for e in range(1, 6)]
    level_b = {s: jnp.where(level(s), 1.0, 0.0).astype(BF16) for s in levels}
    lane_row = lax.broadcasted_iota(jnp.int32, (1, LANES), 1)
    m_lo = jnp.where(lane_row < A_HEAD, 1.0, 0.0).astype(BF16)
    m_hi = jnp.where(lane_row < A_HEAD, 0.0, 1.0).astype(BF16)
    lo_half = lax.broadcasted_iota(jnp.int32, (c64, LANES), 1) < A_HEAD
    stack_heads = lambda x: jnp.concatenate([x * m_lo, x * m_hi], 0)
    twice = lambda x: jnp.concatenate([x, x], 0)
    unstack = lambda x: jnp.where(lo_half, x[:c64], x[c64:])
    bf = lambda x: x.astype(BF16)

    units = [(c, p) for c in range(nck) for p in range(A_PAIRS)]
    group = 8
    for g0 in range(0, len(units), group):
        grp = [g0 + i for i in range(len(units[g0:g0 + group]))]
        sl = {u: (slice(units[u][0] * c64, (units[u][0] + 1) * c64),
                  slice(units[u][1] * LANES, (units[u][1] + 1) * LANES)) for u in grp}
        akk, akb, ark, xs = {}, {}, {}, {}
        for u in grp:
            rs, cs = sl[u]
            a4 = _dot_nt(jnp.concatenate([stack_heads(lkk_ref[rs, cs]), stack_heads(lr_ref[rs, cs])], 0),
                         jnp.concatenate([twice(rhk_ref[rs, cs]), twice(rhb_ref[rs, cs])], 0))
            akk[u] = bf(jnp.where(strict, a4[:LANES, :LANES], 0.0))
            a_kb = jnp.where(strict, a4[:LANES, LANES:], 0.0)
            ark[u] = bf(jnp.where(incl, a4[LANES:, :LANES], 0.0))
            arb_s[u] = bf(jnp.where(incl, a4[LANES:, LANES:], 0.0))
            akb[u] = bf(a_kb)
            xs[u] = eye - jnp.where(level(1), a_kb, 0.0)
        for s in levels:
            xb = {u: bf(xs[u]) for u in grp}
            t1 = {u: bf(_dot(xb[u], akb[u] * level_b[s])) for u in grp}
            xs = {u: xs[u] - _dot(t1[u], xb[u]) for u in grp}
        xb = {u: bf(xs[u]) for u in grp}
        vv = {u: twice(v_ref[sl[u]]) for u in grp}
        for u in grp:
            xlk_s[u] = bf(_dot(xb[u], twice(lkk_ref[sl[u]])))
        av = {u: bf(_dot(akk[u], vv[u])) for u in grp}
        for u in grp:
            xavt_s[u] = _dot(xb[u], av[u]).T
        for u in grp:
            yv_s[u] = unstack(_dot(ark[u], vv[u]))
        for u in grp:
            uv_s[u] = jnp.where(same_head, _dot_tn(v_ref[sl[u]], uk_ref[sl[u]]), 0.0)

    pairs = range(A_PAIRS)
    cols = [slice(p * LANES, (p + 1) * LANES) for p in pairs]
    for c in range(nck):
        rs = slice(c * c64, (c + 1) * c64)
        us = [c * A_PAIRS + p for p in pairs]
        st = [st_ref[0, p] for p in pairs]
        stb = [bf(s_) for s_ in st]
        skt = [bf(jnp.where(same_head, _dot_nt(stb[p], xlk_s[us[p]]) + xavt_s[us[p]], 0.0))
               for p in pairs]
        upd = [_dot(skt[p], stack_heads(ub_ref[rs, cols[p]])) for p in pairs]
        for p in pairs:
            st_ref[0, p] = st[p] * wc_ref[c, :, cols[p]] + uv_s[us[p]] + upd[p]
        g_r = [_dot_nt(lr_ref[rs, cols[p]], stb[p]) for p in pairs]
        y2 = [_dot_nt(arb_s[us[p]], skt[p]) for p in pairs]
        y = jnp.concatenate([g_r[p] + yv_s[us[p]] - (y2[p][:c64] + y2[p][c64:]) for p in pairs], -1)
        mu = _half_sums(y) * (1.0 / A_HEAD)
        d = y - mu
        var = _half_sums(d * d) * (1.0 / A_HEAD)
        yn = d * lax.rsqrt(var + GN_EPS_A) * gng_ref[...] + gnb_ref[...]
        ya_ref[rs, :] = ((yn + bonus_ref[rs, :]) * g_ref[rs, :]).astype(BF16)


def _scan_multi_kernel(lkk_ref, lr_ref, rhk_ref, rhb_ref, uk_ref, ub_ref, v_ref, wc_ref, s0_ref,
                       g_ref, bonus_ref, gng_ref, gnb_ref, ya_ref, sout_ref, st_ref, *, nck, nseg):
    c64 = WKV_CHUNK
    seg = c64 // nseg
    zero = jnp.zeros((A_HEAD, A_HEAD), F32)
    for n in range(nck * nseg):
        for p in range(A_PAIRS):
            st_ref[n, p] = jnp.concatenate([jnp.concatenate([s0_ref[n, 2 * p], zero], 1),
                                            jnp.concatenate([zero, s0_ref[n, 2 * p + 1]], 1)], 0)

    ti = lax.broadcasted_iota(jnp.int32, (LANES, LANES), 0)
    tj = lax.broadcasted_iota(jnp.int32, (LANES, LANES), 1)
    same_head = (ti // c64) == (tj // c64)
    same_seg = (ti // seg) == (tj // seg)
    strict = same_seg & (tj < ti)
    incl = same_seg & (tj <= ti)
    eye = jnp.where(ti == tj, 1.0, 0.0)
    lane_row = lax.broadcasted_iota(jnp.int32, (1, LANES), 1)
    m_lo = jnp.where(lane_row < A_HEAD, 1.0, 0.0).astype(BF16)
    m_hi = jnp.where(lane_row < A_HEAD, 0.0, 1.0).astype(BF16)
    lo_half = lax.broadcasted_iota(jnp.int32, (c64, LANES), 1) < A_HEAD
    stack_heads = lambda x: jnp.concatenate([x * m_lo, x * m_hi], 0)
    twice = lambda x: jnp.concatenate([x, x], 0)
    unstack = lambda x: jnp.where(lo_half, x[:c64], x[c64:])

    bf = lambda x: x.astype(BF16)
    level = lambda s: ((ti // s) % 2 == 1) & ((tj // s) == (ti // s) - 1)
    segs = [slice(s_ * seg, (s_ + 1) * seg) for s_ in range(nseg)]

    units = [(c, p) for c in range(nck) for p in range(A_PAIRS)]
    sl = {u: (slice(u[0] * c64, (u[0] + 1) * c64), slice(u[1] * LANES, (u[1] + 1) * LANES)) for u in units}
    akk, akb, ark, arb, xs = {}, {}, {}, {}, {}
    for u in units:
        rs, cs = sl[u]
        a4 = _dot_nt(jnp.concatenate([stack_heads(lkk_ref[rs, cs]), stack_heads(lr_ref[rs, cs])], 0),
                     jnp.concatenate([twice(rhk_ref[rs, cs]), twice(rhb_ref[rs, cs])], 0))
        akk[u] = bf(jnp.where(strict, a4[:LANES, :LANES], 0.0))
        a_kb = jnp.where(strict, a4[:LANES, LANES:], 0.0)
        ark[u] = bf(jnp.where(incl, a4[LANES:, :LANES], 0.0))
        arb[u] = bf(jnp.where(incl, a4[LANES:, LANES:], 0.0))
        akb[u] = bf(a_kb)
        xs[u] = eye - jnp.where(level(1), a_kb, 0.0)
    s = 2
    while s < seg:
        lvl_b = jnp.where(level(s), 1.0, 0.0).astype(BF16)
        xb = {u: bf(xs[u]) for u in units}
        t1 = {u: bf(_dot(xb[u], akb[u] * lvl_b)) for u in units}
        xs = {u: xs[u] - _dot(t1[u], xb[u]) for u in units}
        s *= 2
    xb = {u: bf(xs[u]) for u in units}
    vv = {u: twice(v_ref[sl[u]]) for u in units}

    lkf = {u: lkk_ref[sl[u]].astype(F32) for u in units}
    lrf = {u: lr_ref[sl[u]].astype(F32) for u in units}
    g_k = {(c, p): jnp.concatenate([_dot_nt(lkf[c, p][ss], st_ref[c * nseg + i, p])
                                    for i, ss in enumerate(segs)], 0) for c, p in units}
    g_r = {(c, p): jnp.concatenate([_dot_nt(lrf[c, p][ss], st_ref[c * nseg + i, p])
                                    for i, ss in enumerate(segs)], 0) for c, p in units}
    rhs = {u: bf(twice(g_k[u]) + _dot(akk[u], vv[u])) for u in units}
    sk = {u: unstack(_dot(xb[u], rhs[u])) for u in units}
    y2 = {u: _dot(ark[u], vv[u]) - _dot(arb[u], twice(bf(sk[u]))) for u in units}
    ys = {u: g_r[u] + unstack(y2[u]) for u in units}
    for c, p in units:
        rs, cs = sl[c, p]
        vf, ukf, ubf = v_ref[rs, cs].astype(F32), uk_ref[rs, cs].astype(F32), ub_ref[rs, cs].astype(F32)
        for i, ss in enumerate(segs):
            upd = _dot_tn(jnp.concatenate([vf[ss], sk[c, p][ss]], 0),
                          jnp.concatenate([ukf[ss], ubf[ss]], 0))
            n = c * nseg + i
            new = st_ref[n, p] * wc_ref[n, :, cs] + upd
            sout_ref[n, 2 * p] = new[:A_HEAD, :A_HEAD]
            sout_ref[n, 2 * p + 1] = new[A_HEAD:, A_HEAD:]

    for c in range(nck):
        rs = slice(c * c64, (c + 1) * c64)
        y = jnp.concatenate([ys[c, p] for p in range(A_PAIRS)], -1)
        mu = _half_sums(y) * (1.0 / A_HEAD)
        d = y - mu
        var = _half_sums(d * d) * (1.0 / A_HEAD)
        yn = d * lax.rsqrt(var + GN_EPS_A) * gng_ref[...] + gnb_ref[...]
        ya_ref[rs, :] = ((yn + bonus_ref[rs, :]) * g_ref[rs, :]).astype(BF16)


def _scan(pre, s0, layer, gn_g, gn_b, nb, nj, nck, nseg, carry):
    lkk, lr, rhk, rhb, uk, ub, v, wc, g, bonus = pre
    t = lkk.shape[0]
    rows = nck * WKV_CHUNK
    blk = pl.BlockSpec((rows, A_WIDTH), lambda i, j: (i * nj + j, 0))
    wcs = pl.BlockSpec((nck * nseg, 1, A_WIDTH), lambda i, j: (i * nj + j, 0, 0))
    vec = pl.BlockSpec((1, A_WIDTH), lambda i, j: (0, 0))
    if carry:
        units = nck * A_PAIRS
        body = functools.partial(_scan_seq_kernel, nck=nck)
        s_in = s_out = pl.BlockSpec((1, A_PAIRS, LANES, LANES), lambda i, j: (i, 0, 0, 0))
        s_shape = s0.shape
        scratch = [pltpu.VMEM((units, LANES, LANES), BF16), pltpu.VMEM((units, LANES, LANES), F32),
                   pltpu.VMEM((units, LANES, LANES), BF16), pltpu.VMEM((units, WKV_CHUNK, LANES), F32),
                   pltpu.VMEM((units, LANES, LANES), F32)]
    else:
        nst = nck * nseg
        body = functools.partial(_scan_multi_kernel, nck=nck, nseg=nseg)
        s_in = pl.BlockSpec((None, nst, A_HEADS, A_HEAD, A_HEAD), lambda i, j: (layer, i, 0, 0, 0))
        s_out = pl.BlockSpec((nst, A_HEADS, A_HEAD, A_HEAD), lambda i, j: (i, 0, 0, 0))
        s_shape = s0.shape[1:]
        scratch = [pltpu.VMEM((nst, A_PAIRS, LANES, LANES), F32)]
    return pl.pallas_call(
        body,
        grid=(nb, nj),
        in_specs=[blk] * 7 + [wcs, s_in, blk, blk, vec, vec],
        out_specs=[blk, s_out],
        scratch_shapes=scratch,
        out_shape=[jax.ShapeDtypeStruct((t, A_WIDTH), BF16), jax.ShapeDtypeStruct(s_shape, F32)],
        compiler_params=pltpu.CompilerParams(dimension_semantics=("parallel", "arbitrary"),
                                             vmem_limit_bytes=VMEM_LIMIT),
        name="rwkv_scan",
    )(lkk, lr, rhk, rhb, uk, ub, v, wc, s0, g, bonus, gn_g, gn_b)


def _ret_kernel(q_ref, k_ref, v_ref, gate_ref, cos_ref, sin_ref, din_ref, dq_ref, dk_ref, dc_ref,
                s0_ref, gng_ref, gnb_ref, yb_ref, st_ref, *, nck, nseg, carry):
    ch = RET_CHUNK
    seg = ch // nseg
    if carry:
        @pl.when(pl.program_id(1) == 0)
        def _():
            st_ref[...] = s0_ref[...]
    else:
        st_ref[...] = s0_ref[...]

    lane = lax.broadcasted_iota(jnp.int32, (ch, B_QK), 1)
    first = (lane % B_DK) < (B_DK // 2)
    lane_row = lax.broadcasted_iota(jnp.int32, (1, LANES), 1)
    masks = [jnp.where(lane_row < B_DK, 1.0, 0.0), jnp.where(lane_row < B_DK, 0.0, 1.0)]

    def rotary(x, cos, sin):
        swapped = jnp.where(first, pltpu.roll(x, B_QK - B_DK // 2, axis=1),
                            pltpu.roll(x, B_DK // 2, axis=1))
        return x * cos + swapped * sin

    for c in range(nck):
        rs = slice(c * ch, (c + 1) * ch)
        cos = jnp.concatenate([cos_ref[rs, :]] * (B_QK // LANES), -1)
        sin = jnp.concatenate([sin_ref[rs, :]] * (B_QK // LANES), -1)
        q = rotary(q_ref[rs, :], cos, sin)
        k = rotary(k_ref[rs, :], cos, sin) * (B_DK ** -0.5)
        for h in range(B_HEADS):
            ps = slice((h // 2) * LANES, (h // 2 + 1) * LANES)
            hs = slice(h * B_DV, (h + 1) * B_DV)
            qp = q[:, ps]
            km = k[:, ps] * masks[h % 2]
            vb = v_ref[rs, hs].astype(BF16)
            scores = _dot_nt(qp.astype(BF16), km.astype(BF16)) * din_ref[h]
            o = _dot(scores.astype(BF16), vb)
            qd = qp * dq_ref[h]
            kd = km * dk_ref[h]
            dc = dc_ref[h][0:1, :]
            own = slice((h % 2) * B_DK, (h % 2 + 1) * B_DK)
            zero = jnp.zeros((B_DK, B_DV), F32)
            pad = lambda s_: jnp.concatenate([s_, zero] if h % 2 == 0 else [zero, s_], 0)
            if nseg == 1:
                st = st_ref[0, h]
                o = o + _dot(qd.astype(BF16), pad(st).astype(BF16))
                st_ref[0, h] = st * dc + _dot_tn(kd.astype(BF16), vb)[own]
            else:
                vf = v_ref[rs, hs]
                cross = []
                for s_ in range(nseg):
                    ss = slice(s_ * seg, (s_ + 1) * seg)
                    i = c * nseg + s_
                    st = st_ref[i, h]
                    cross.append(_dot(qd[ss], pad(st)))
                    st_ref[i, h] = st * dc + _dot_tn(kd[ss], vf[ss])[own]
                o = o + jnp.concatenate(cross, 0)
            mu = jnp.mean(o, -1, keepdims=True)
            d = o - mu
            var = jnp.mean(d * d, -1, keepdims=True)
            yn = d * lax.rsqrt(var + GN_EPS_B) * gng_ref[:, hs] + gnb_ref[:, hs]
            gate = gate_ref[rs, hs]
            yb_ref[rs, hs] = (yn * (gate * jax.nn.sigmoid(gate))).astype(BF16)


def _retention(zb, tabs, s0, layer, gn_g, gn_b, nb, nj, nck, nseg, carry, tab_rows_fixed):
    cos, sin, din, dq, dk, dc = tabs
    t = zb.shape[0]
    rows = nck * RET_CHUNK
    nst = s0.shape[1] // nb
    qs = pl.BlockSpec((rows, B_QK), lambda i, j: (i * nj + j, 0))
    ks = pl.BlockSpec((rows, B_QK), lambda i, j: (i * nj + j, 1))
    vs = pl.BlockSpec((rows, B_V), lambda i, j: (i * nj + j, 1))
    gs = pl.BlockSpec((rows, B_V), lambda i, j: (i * nj + j, 2))
    if tab_rows_fixed:
        tab = pl.BlockSpec((rows, LANES), lambda i, j: (0, 0))
    else:
        tab = pl.BlockSpec((rows, LANES), lambda i, j: (j, 0))
    s_in = pl.BlockSpec((None, nst, B_HEADS, B_DK, B_DV), lambda i, j: (layer, i, 0, 0, 0))
    sts = pl.BlockSpec((nst, B_HEADS, B_DK, B_DV), lambda i, j: (i, 0, 0, 0))
    const = lambda a: pl.BlockSpec(a.shape, lambda i, j: (0,) * a.ndim)
    return pl.pallas_call(
        functools.partial(_ret_kernel, nck=nck, nseg=nseg, carry=carry),
        grid=(nb, nj),
        in_specs=[qs, ks, vs, gs, tab, tab, const(din), const(dq), const(dk), const(dc), s_in,
                  const(gn_g), const(gn_b)],
        out_specs=[pl.BlockSpec((rows, B_V), lambda i, j: (i * nj + j, 0)), sts],
        out_shape=[jax.ShapeDtypeStruct((t, B_V), BF16), jax.ShapeDtypeStruct(s0.shape[1:], F32)],
        compiler_params=pltpu.CompilerParams(dimension_semantics=("parallel", "arbitrary"),
                                             vmem_limit_bytes=VMEM_LIMIT),
        name="retention",
    )(zb, zb, zb, zb, cos, sin, din, dq, dk, dc, s0, gn_g, gn_b)


def _merge_ffn_kernel(x_ref, ya_ref, yb_ref, zg_ref, woa_ref, wob_ref, wo_ref, gpost_ref, gpre_ref,
                      wgate_ref, wup_ref, wdown_ref, gfpost_ref, out_ref):
    zg = zg_ref[...]
    m = (jax.nn.sigmoid(zg[:, :D_MODEL]) * _dot(ya_ref[...], woa_ref[...])
         + jax.nn.sigmoid(zg[:, D_MODEL:]) * _dot(yb_ref[...], wob_ref[...]))
    x1 = x_ref[...] + _rms(_dot(m.astype(BF16), wo_ref[...]), gpost_ref[...])
    hb = _rms(x1, gpre_ref[...]).astype(BF16)
    gate = _dot(hb, wgate_ref[...])
    up = _dot(hb, wup_ref[...])
    f = _dot((gate * jax.nn.sigmoid(gate) * up).astype(BF16), wdown_ref[...])
    out_ref[...] = x1 + _rms(f, gfpost_ref[...])


def _merge_ffn(x, ya, yb, zg, mp, tm):
    t = x.shape[0]
    row = lambda n: pl.BlockSpec((tm, n), lambda i: (i, 0))
    ws = [mp['w_out_a'], mp['w_out_b'], mp['w_o'], mp['g_post'], mp['g_ffn_pre'], mp['w_gate'],
          mp['w_up'], mp['w_down'], mp['g_ffn_post']]
    return pl.pallas_call(
        _merge_ffn_kernel,
        grid=(t // tm,),
        in_specs=[row(D_MODEL), row(A_WIDTH), row(B_V), row(2 * D_MODEL)] +
                 [_resident(w.shape) for w in ws],
        out_specs=row(D_MODEL),
        out_shape=jax.ShapeDtypeStruct((t, D_MODEL), F32),
        compiler_params=pltpu.CompilerParams(dimension_semantics=("parallel",),
                                             vmem_limit_bytes=VMEM_LIMIT),
        name="merge_ffn",
    )(x, ya, yb, zg, *ws)


def _layer_params(l, w_in, mu_shift, w0, lora_w_up, a0, lora_a_up, lora_g_up, k_k, k_a, r_k):
    wi = w_in[l]
    row = lambda a: a.reshape(1, -1)
    wlora = jnp.zeros((LANES, 2 * A_WIDTH), F32)
    wlora = wlora.at[:LORA_W, :A_WIDTH].set(lora_w_up[l]).at[LORA_W:, A_WIDTH:].set(lora_a_up[l])
    lg = jnp.zeros((SHIFT_PAD - 3 * A_WIDTH - LANES, A_WIDTH), F32).at[:LORA_G].set(lora_g_up[l])
    return {
        'w_in': jnp.concatenate([wi[:, :SHIFT_W].astype(BF16),
                                 jnp.zeros((D_MODEL, SHIFT_PAD - SHIFT_W), BF16),
                                 wi[:, SHIFT_W:].astype(BF16)], 1),
        'mu': jnp.pad(row(mu_shift[l]), ((0, 0), (0, SHIFT_PAD - SHIFT_W))),
        'wlora': wlora.astype(BF16), 'lg': lg.astype(BF16),
        'w0': row(w0[l]), 'a0': row(a0[l]), 'k_k': row(k_k[l]), 'k_a': row(k_a[l]), 'r_k': row(r_k[l]),
    }


def _ret_tables(pos, seg, rows):
    half = B_DK // 2
    inv = ROPE_BASE ** (-jnp.arange(half, dtype=F32) / half)
    ang = pos.astype(F32)[:, None] * inv[None, :]
    cos, sin = jnp.cos(ang), jnp.sin(ang)
    cos = jnp.tile(jnp.concatenate([cos, cos], -1), (1, LANES // B_DK))
    sin = jnp.tile(jnp.concatenate([-sin, sin], -1), (1, LANES // B_DK))
    log_g = jnp.log(1.0 - jnp.exp2(-5.0 - jnp.arange(B_HEADS, dtype=F32)))
    idx = jnp.arange(rows)
    loc = (idx % seg).astype(F32)
    rel = loc[:, None] - loc[None, :]
    ok = ((idx[:, None] // seg) == (idx[None, :] // seg)) & (rel >= 0)
    din = jnp.where(ok[None], jnp.exp(log_g[:, None, None] * jnp.where(ok, rel, 0.0)[None]), 0.0)
    bc = lambda col: jnp.broadcast_to(col[:, :, None], (B_HEADS, rows, LANES))
    dq = bc(jnp.exp(log_g[:, None] * (loc[None, :] + 1.0)))
    dk = bc(jnp.exp(log_g[:, None] * (seg - 1.0 - loc[None, :])))
    dc = jnp.broadcast_to(jnp.exp(log_g * seg)[:, None, None], (B_HEADS, 8, LANES))
    return cos, sin, din, dq, dk, dc


def _extract_wkv(s):
    b = s.shape[0]
    return jnp.stack([s[:, :, :A_HEAD, :A_HEAD], s[:, :, A_HEAD:, A_HEAD:]], 2).reshape(
        b, A_HEADS, A_HEAD, A_HEAD)


def _group_layer(x, nseq, seq_len, pos0, states, layer, lp, mp, gn):
    t = x.shape[0]
    long_seq = seq_len >= RET_CHUNK
    tm = 256
    za, zb, zg = _in_proj(x, lp['g_pre'], lp['w_in'], tm)

    if long_seq:
        wkv_nseg, ret_nseg = 1, 1
        nck_a, nck_b = 4, 2
        prep_rows = 512
        nb, nj_a, nj_b = nseq, seq_len // (nck_a * WKV_CHUNK), seq_len // (nck_b * RET_CHUNK)
        pos = pos0 + jnp.arange(seq_len)
        tabs = _ret_tables(pos, RET_CHUNK, RET_CHUNK)
    else:
        wkv_nseg, ret_nseg = WKV_CHUNK // seq_len, RET_CHUNK // seq_len
        nck_a, nck_b = 2, 1
        prep_rows = RET_CHUNK
        nb, nj_a, nj_b = t // (nck_a * WKV_CHUNK), 1, 1
        pos = pos0 + (jnp.arange(RET_CHUNK) % seq_len)
        tabs = _ret_tables(pos, seq_len, RET_CHUNK)
    nb_b = t // (nck_b * RET_CHUNK * nj_b)

    if states is None:
        s_wkv = jnp.zeros((nseq, A_PAIRS, LANES, LANES), F32)
        s_ret = jnp.zeros((1, nseq, B_HEADS, B_DK, B_DV), F32)
        shift_rows = None
        layer = 0
    else:
        s_wkv, s_ret = states[0], states[1]
        shift_rows = jnp.repeat(jnp.pad(states[2], ((0, 0), (0, SHIFT_PAD - SHIFT_W))), seq_len, axis=0)

    pre = _prep(za, shift_rows, lp, seq_len, min(seq_len, WKV_CHUNK), prep_rows)
    ya, wkv = _scan(pre, s_wkv, layer, gn['a_g'], gn['a_b'], nb, nj_a, nck_a, wkv_nseg, long_seq)
    yb, ret = _retention(zb, tabs, s_ret, layer, gn['b_g'], gn['b_b'], nb_b, nj_b, nck_b, ret_nseg,
                         long_seq, not long_seq)
    x = _merge_ffn(x, ya, yb, zg, mp, tm)
    shift = za.reshape(nseq, seq_len, SHIFT_PAD)[:, -1, :SHIFT_W]
    return x, (_extract_wkv(wkv) if long_seq else wkv), ret, shift


def kernel(x_prompt, x_sample, state_wkv, state_ret, state_shift, norm_mix_pre, w_in, mu_shift, w0,
           lora_w_up, a0, lora_a_up, lora_g_up, k_k, k_a, r_k, gn_a_gain, gn_a_bias, w_out_a,
           gn_b_gain, gn_b_bias, w_out_b, w_o, norm_mix_post, norm_ffn_pre, w_ffn_gate, w_ffn_up,
           w_ffn_down, norm_ffn_post):
    bp, lp_, _ = x_prompt.shape
    bs, ls_, _ = x_sample.shape
    depth = w_in.shape[0]
    yp = x_prompt.reshape(bp * lp_, D_MODEL)
    ys = x_sample.reshape(bs * ls_, D_MODEL)
    row = lambda a: a.reshape(1, -1)
    outs = [[] for _ in range(6)]
    for l in range(depth):
        lp = _layer_params(l, w_in, mu_shift, w0, lora_w_up, a0, lora_a_up, lora_g_up, k_k, k_a, r_k)
        lp['g_pre'] = row(norm_mix_pre[l])
        mp = {'w_out_a': w_out_a[l].astype(BF16), 'w_out_b': w_out_b[l].astype(BF16),
              'w_o': w_o[l].astype(BF16), 'g_post': row(norm_mix_post[l]),
              'g_ffn_pre': row(norm_ffn_pre[l]), 'w_gate': w_ffn_gate[l].astype(BF16),
              'w_up': w_ffn_up[l].astype(BF16), 'w_down': w_ffn_down[l].astype(BF16),
              'g_ffn_post': row(norm_ffn_post[l])}
        gn = {'a_g': row(gn_a_gain[l]), 'a_b': row(gn_a_bias[l]),
              'b_g': row(gn_b_gain[l]), 'b_b': row(gn_b_bias[l])}
        yp, a_, b_, c_ = _group_layer(yp, bp, lp_, 0, None, l, lp, mp, gn)
        outs[0].append(a_); outs[1].append(b_); outs[2].append(c_)
        ys, a_, b_, c_ = _group_layer(ys, bs, ls_, PAST_LEN,
                                      (state_wkv, state_ret, state_shift[l]), l, lp, mp, gn)
        outs[3].append(a_); outs[4].append(b_); outs[5].append(c_)
    return (yp.reshape(bp, lp_, D_MODEL), ys.reshape(bs, ls_, D_MODEL)) + \
        tuple(jnp.stack(o) for o in outs)
```

```python
import functools

import jax
import jax.numpy as jnp
from jax import lax
from jax.experimental import pallas as pl
from jax.experimental.pallas import tpu as pltpu

F32 = jnp.float32
BF16 = jnp.bfloat16

LANES = 128
D_MODEL = 1024
PAST_LEN = 16384
A_HEADS = 8
A_HEAD = 64
A_WIDTH = A_HEADS * A_HEAD
A_PAIRS = A_WIDTH // LANES
LORA_W = 64
LORA_A = 64
LORA_G = 160
SHIFT_W = 3 * A_WIDTH + LORA_W + LORA_A + LORA_G
SHIFT_PAD = 1920
GN_EPS_A = 64e-5
B_HEADS = 8
B_DK = 64
B_DV = 128
B_QK = B_HEADS * B_DK
B_V = B_HEADS * B_DV
RET_W = 2 * B_QK + 2 * B_V
RET_CHUNK = 128
ROPE_BASE = 10000.0
GN_EPS_B = 1e-5
D_FF = 2816
RMS_EPS = 1e-6
WKV_CHUNK = 64

VMEM_LIMIT = 56 * 1024 * 1024


def _resident(shape):
    nd = len(shape)
    return pl.BlockSpec(shape, lambda *_: (0,) * nd, pipeline_mode=pl.Buffered(1))


def _dot(a, b):
    return jnp.dot(a, b, preferred_element_type=F32)


def _dot_nt(a, b):
    return lax.dot_general(a, b, (((1,), (1,)), ((), ())), preferred_element_type=F32)


def _dot_tn(a, b):
    return lax.dot_general(a, b, (((0,), (0,)), ((), ())), preferred_element_type=F32)


def _rms(x, g):
    return x * lax.rsqrt(jnp.mean(x * x, -1, keepdims=True) + RMS_EPS) * g


def _half_sums(x):
    rows, width = x.shape
    lo = lax.broadcasted_iota(jnp.int32, (rows, LANES), 1) < A_HEAD
    out = []
    for p in range(width // LANES):
        xp = x[:, p * LANES:(p + 1) * LANES]
        s_lo = jnp.sum(jnp.where(lo, xp, 0.0), -1, keepdims=True)
        s_hi = jnp.sum(jnp.where(lo, 0.0, xp), -1, keepdims=True)
        out.append(jnp.where(lo, s_lo, s_hi))
    return jnp.concatenate(out, -1)


def _sub_rows(n, nsub):
    sub = n // nsub
    return [slice(i * sub, (i + 1) * sub) for i in range(nsub)]


def _in_proj_kernel(x_ref, g_ref, w_ref, za_ref, zb_ref, zg_ref, *, nsub):
    rows = _sub_rows(x_ref.shape[0], nsub)
    hb = [_rms(x_ref[r, :], g_ref[...]).astype(BF16) for r in rows]
    col0 = 0
    for out_ref in (za_ref, zb_ref, zg_ref):
        cols = slice(col0, col0 + out_ref.shape[1])
        col0 = cols.stop
        for r, h in zip(rows, hb):
            out_ref[r, :] = _dot(h, w_ref[:, cols]).astype(out_ref.dtype)


def _in_proj(x, g, w, tm, nsub):
    t = x.shape[0]
    row = lambda n: pl.BlockSpec((tm, n), lambda i: (i, 0))
    return pl.pallas_call(
        functools.partial(_in_proj_kernel, nsub=nsub),
        grid=(t // tm,),
        in_specs=[row(D_MODEL), _resident(g.shape), _resident(w.shape)],
        out_specs=[row(SHIFT_PAD), row(RET_W), row(2 * D_MODEL)],
        out_shape=[jax.ShapeDtypeStruct((t, SHIFT_PAD), F32), jax.ShapeDtypeStruct((t, RET_W), BF16),
                   jax.ShapeDtypeStruct((t, 2 * D_MODEL), BF16)],
        compiler_params=pltpu.CompilerParams(dimension_semantics=("parallel",),
                                             vmem_limit_bytes=VMEM_LIMIT),
        name="in_proj",
    )(x, g, w)


def _softplus(y):
    return jnp.maximum(y, 0.0) + jnp.log(1.0 + jnp.exp(-jnp.abs(y)))


def _prep_kernel(*refs, seq_len, seg, has_shift):
    if has_shift:
        za_ref, halo_ref, shift_ref = refs[:3]
        refs = refs[3:]
    else:
        za_ref, halo_ref = refs[:2]
        shift_ref = None
        refs = refs[2:]
    (mu_ref, wlora_ref, lg_ref, w0_ref, a0_ref, kk_ref, ka_ref, rk_ref,
     lkk_ref, lr_ref, rhk_ref, rhb_ref, uk_ref, ub_ref, v_ref, wc_ref, g_ref, bonus_ref) = refs

    za = za_ref[...]
    rows = za.shape[0]
    row = lax.broadcasted_iota(jnp.int32, (rows, 1), 0)
    prev = pltpu.roll(za, 1, axis=0)
    prev = jnp.where(row == 0, halo_ref[7:8, :], prev)
    if has_shift:
        prev = jnp.where(row % seq_len == 0, shift_ref[...], prev)
    else:
        starts = (pl.program_id(0) * rows) % seq_len == 0
        prev = jnp.where((row == 0) & starts, 0.0, prev)
    zs = za + mu_ref[...] * (prev - za)

    r = zs[:, 0:A_WIDTH]
    k = zs[:, A_WIDTH:2 * A_WIDTH]
    v = zs[:, 2 * A_WIDTH:3 * A_WIDTH]
    dwa = zs[:, 3 * A_WIDTH:3 * A_WIDTH + LANES]
    dg = zs[:, 3 * A_WIDTH + LANES:SHIFT_PAD]
    lane = lax.broadcasted_iota(jnp.int32, (rows, LANES), 1)
    act = jnp.where(lane < LORA_W, jnp.tanh(dwa), dwa).astype(BF16)
    lora = _dot(act, wlora_ref[...])
    w = -_softplus(-(w0_ref[...] + lora[:, :A_WIDTH])) - 0.5
    lw = -jnp.exp(w)
    a = jax.nn.sigmoid(a0_ref[...] + lora[:, A_WIDTH:])
    g_ref[...] = _dot(jax.nn.sigmoid(dg).astype(BF16), lg_ref[...])

    kk = k * kk_ref[...]
    kk = kk * lax.rsqrt(jnp.maximum(_half_sums(kk * kk), 1e-24))
    k2 = k * (1.0 + (a - 1.0) * ka_ref[...])
    b = kk * a
    bonus_ref[...] = _half_sums(r * k2 * rk_ref[...]) * v
    v_ref[...] = v.astype(BF16)

    ti = lax.broadcasted_iota(jnp.int32, (LANES, LANES), 0)
    tj = lax.broadcasted_iota(jnp.int32, (LANES, LANES), 1)
    same = (ti // seg) == (tj // seg)
    tri = jnp.concatenate([jnp.where(same & (tj <= ti), 1.0, 0.0),
                           jnp.where(same, 1.0, 0.0)], 0).astype(BF16)
    for g0 in range(0, rows, LANES):
        sl = slice(g0, g0 + LANES)
        x = lw[sl]
        hi = x.astype(BF16)
        r1 = x - hi.astype(F32)
        mid = r1.astype(BF16)
        lo = (r1 - mid.astype(F32)).astype(BF16)
        ct = _dot(tri, hi) + _dot(tri, mid) + _dot(tri, lo)
        cum, tot = ct[:LANES], ct[LANES:]
        inv = jnp.exp(-cum)
        tail = jnp.exp(tot - cum)
        lkk_ref[sl, :] = (kk[sl] * jnp.exp(cum - x)).astype(BF16)
        lr_ref[sl, :] = (r[sl] * jnp.exp(cum)).astype(BF16)
        rhk_ref[sl, :] = (k2[sl] * inv).astype(BF16)
        rhb_ref[sl, :] = (b[sl] * inv).astype(BF16)
        uk_ref[sl, :] = (k2[sl] * tail).astype(BF16)
        ub_ref[sl, :] = (-(b[sl] * tail)).astype(BF16)
        for s in range(LANES // seg):
            wc_ref[g0 // seg + s] = jnp.exp(tot[s * seg:s * seg + 1, :])


def _prep(za, shift_rows, pp, seq_len, seg, rows):
    t = za.shape[0]
    has_shift = shift_rows is not None
    blk = lambda n: pl.BlockSpec((rows, n), lambda i: (i, 0))
    halo = pl.BlockSpec((8, SHIFT_PAD), lambda i: (jnp.maximum(i * (rows // 8) - 1, 0), 0))
    params = [pp['mu'], pp['wlora'], pp['lg'], pp['w0'], pp['a0'], pp['k_k'], pp['k_a'], pp['r_k']]
    ins = [za, za] + ([shift_rows] if has_shift else []) + params
    in_specs = [blk(SHIFT_PAD), halo] + ([blk(SHIFT_PAD)] if has_shift else []) + \
        [_resident(p.shape) for p in params]
    bf = jax.ShapeDtypeStruct((t, A_WIDTH), BF16)
    f32 = jax.ShapeDtypeStruct((t, A_WIDTH), F32)
    out_shape = [bf] * 7 + [jax.ShapeDtypeStruct((t // seg, 1, A_WIDTH), F32), f32, f32]
    out_specs = [blk(A_WIDTH)] * 7 + [pl.BlockSpec((rows // seg, 1, A_WIDTH), lambda i: (i, 0, 0)),
                                      blk(A_WIDTH), blk(A_WIDTH)]
    return pl.pallas_call(
        functools.partial(_prep_kernel, seq_len=seq_len, seg=seg, has_shift=has_shift),
        grid=(t // rows,),
        in_specs=in_specs, out_specs=out_specs, out_shape=out_shape,
        compiler_params=pltpu.CompilerParams(dimension_semantics=("parallel",),
                                             vmem_limit_bytes=VMEM_LIMIT),
        name="rwkv_prep",
    )(*ins)


def _scan_seq_kernel(lkk_ref, lr_ref, rhk_ref, rhb_ref, uk_ref, ub_ref, v_ref, wc_ref, s0_ref,
                     g_ref, bonus_ref, gng_ref, gnb_ref, ya_ref, st_ref,
                     xlk_s, xavt_s, arb_s, yv_s, uv_s, *, nck):
    c64 = WKV_CHUNK

    @pl.when(pl.program_id(1) == 0)
    def _():
        st_ref[...] = s0_ref[...]

    ti = lax.broadcasted_iota(jnp.int32, (LANES, LANES), 0)
    tj = lax.broadcasted_iota(jnp.int32, (LANES, LANES), 1)
    same_head = (ti // c64) == (tj // c64)
    strict = same_head & (tj < ti)
    incl = same_head & (tj <= ti)
    eye = jnp.where(ti == tj, 1.0, 0.0)
    level = lambda s: ((ti // s) % 2 == 1) & ((tj // s) == (ti // s) - 1)
    levels = [2 ** e for e in range(1, 6)]
    level_b = {s: jnp.where(level(s), 1.0, 0.0).astype(BF16) for s in levels}
    lane_row = lax.broadcasted_iota(jnp.int32, (1, LANES), 1)
    m_lo = jnp.where(lane_row < A_HEAD, 1.0, 0.0).astype(BF16)
    m_hi = jnp.where(lane_row < A_HEAD, 0.0, 1.0).astype(BF16)
    lo_half = lax.broadcasted_iota(jnp.int32, (c64, LANES), 1) < A_HEAD
    stack_heads = lambda x: jnp.concatenate([x * m_lo, x * m_hi], 0)
    twice = lambda x: jnp.concatenate([x, x], 0)
    unstack = lambda x: jnp.where(lo_half, x[:c64], x[c64:])
    bf = lambda x: x.astype(BF16)

    units = [(c, p) for c in range(nck) for p in range(A_PAIRS)]
    group = 8
    for g0 in range(0, len(units), group):
        grp = [g0 + i for i in range(len(units[g0:g0 + group]))]
        sl = {u: (slice(units[u][0] * c64, (units[u][0] + 1) * c64),
                  slice(units[u][1] * LANES, (units[u][1] + 1) * LANES)) for u in grp}
        akk, akb, ark, xs = {}, {}, {}, {}
        for u in grp:
            rs, cs = sl[u]
            a4 = _dot_nt(jnp.concatenate([stack_heads(lkk_ref[rs, cs]), stack_heads(lr_ref[rs, cs])], 0),
                         jnp.concatenate([twice(rhk_ref[rs, cs]), twice(rhb_ref[rs, cs])], 0))
            akk[u] = bf(jnp.where(strict, a4[:LANES, :LANES], 0.0))
            a_kb = jnp.where(strict, a4[:LANES, LANES:], 0.0)
            ark[u] = bf(jnp.where(incl, a4[LANES:, :LANES], 0.0))
            arb_s[u] = bf(jnp.where(incl, a4[LANES:, LANES:], 0.0))
            akb[u] = bf(a_kb)
            xs[u] = eye - jnp.where(level(1), a_kb, 0.0)
        for s in levels:
            xb = {u: bf(xs[u]) for u in grp}
            t1 = {u: bf(_dot(xb[u], akb[u] * level_b[s])) for u in grp}
            xs = {u: xs[u] - _dot(t1[u], xb[u]) for u in grp}
        xb = {u: bf(xs[u]) for u in grp}
        vv = {u: twice(v_ref[sl[u]]) for u in grp}
        for u in grp:
            xlk_s[u] = bf(_dot(xb[u], twice(lkk_ref[sl[u]])))
        av = {u: bf(_dot(akk[u], vv[u])) for u in grp}
        for u in grp:
            xavt_s[u] = _dot(xb[u], av[u]).T
        for u in grp:
            yv_s[u] = unstack(_dot(ark[u], vv[u]))
        for u in grp:
            uv_s[u] = jnp.where(same_head, _dot_tn(v_ref[sl[u]], uk_ref[sl[u]]), 0.0)

    pairs = range(A_PAIRS)
    cols = [slice(p * LANES, (p + 1) * LANES) for p in pairs]
    for c in range(nck):
        rs = slice(c * c64, (c + 1) * c64)
        us = [c * A_PAIRS + p for p in pairs]
        st = [st_ref[0, p] for p in pairs]
        stb = [bf(s_) for s_ in st]
        skt = [bf(jnp.where(same_head, _dot_nt(stb[p], xlk_s[us[p]]) + xavt_s[us[p]], 0.0))
               for p in pairs]
        upd = [_dot(skt[p], stack_heads(ub_ref[rs, cols[p]])) for p in pairs]
        for p in pairs:
            st_ref[0, p] = st[p] * wc_ref[c, :, cols[p]] + uv_s[us[p]] + upd[p]
        g_r = [_dot_nt(lr_ref[rs, cols[p]], stb[p]) for p in pairs]
        y2 = [_dot_nt(arb_s[us[p]], skt[p]) for p in pairs]
        y = jnp.concatenate([g_r[p] + yv_s[us[p]] - (y2[p][:c64] + y2[p][c64:]) for p in pairs], -1)
        mu = _half_sums(y) * (1.0 / A_HEAD)
        d = y - mu
        var = _half_sums(d * d) * (1.0 / A_HEAD)
        yn = d * lax.rsqrt(var + GN_EPS_A) * gng_ref[...] + gnb_ref[...]
        ya_ref[rs, :] = ((yn + bonus_ref[rs, :]) * g_ref[rs, :]).astype(BF16)


def _scan_multi_kernel(lkk_ref, lr_ref, rhk_ref, rhb_ref, uk_ref, ub_ref, v_ref, wc_ref, s0_ref,
                       g_ref, bonus_ref, gng_ref, gnb_ref, ya_ref, sout_ref, st_ref, *, nck, nseg):
    c64 = WKV_CHUNK
    seg = c64 // nseg
    zero = jnp.zeros((A_HEAD, A_HEAD), F32)
    for n in range(nck * nseg):
        for p in range(A_PAIRS):
            st_ref[n, p] = jnp.concatenate([jnp.concatenate([s0_ref[n, 2 * p], zero], 1),
                                            jnp.concatenate([zero, s0_ref[n, 2 * p + 1]], 1)], 0)

    ti = lax.broadcasted_iota(jnp.int32, (LANES, LANES), 0)
    tj = lax.broadcasted_iota(jnp.int32, (LANES, LANES), 1)
    same_head = (ti // c64) == (tj // c64)
    same_seg = (ti // seg) == (tj // seg)
    strict = same_seg & (tj < ti)
    incl = same_seg & (tj <= ti)
    eye = jnp.where(ti == tj, 1.0, 0.0)
    lane_row = lax.broadcasted_iota(jnp.int32, (1, LANES), 1)
    m_lo = jnp.where(lane_row < A_HEAD, 1.0, 0.0).astype(BF16)
    m_hi = jnp.where(lane_row < A_HEAD, 0.0, 1.0).astype(BF16)
    lo_half = lax.broadcasted_iota(jnp.int32, (c64, LANES), 1) < A_HEAD
    stack_heads = lambda x: jnp.concatenate([x * m_lo, x * m_hi], 0)
    twice = lambda x: jnp.concatenate([x, x], 0)
    unstack = lambda x: jnp.where(lo_half, x[:c64], x[c64:])

    bf = lambda x: x.astype(BF16)
    level = lambda s: ((ti // s) % 2 == 1) & ((tj // s) == (ti // s) - 1)
    segs = [slice(s_ * seg, (s_ + 1) * seg) for s_ in range(nseg)]

    units = [(c, p) for c in range(nck) for p in range(A_PAIRS)]
    sl = {u: (slice(u[0] * c64, (u[0] + 1) * c64), slice(u[1] * LANES, (u[1] + 1) * LANES)) for u in units}
    akk, akb, ark, arb, xs = {}, {}, {}, {}, {}
    for u in units:
        rs, cs = sl[u]
        a4 = _dot_nt(jnp.concatenate([stack_heads(lkk_ref[rs, cs]), stack_heads(lr_ref[rs, cs])], 0),
                     jnp.concatenate([twice(rhk_ref[rs, cs]), twice(rhb_ref[rs, cs])], 0))
        akk[u] = bf(jnp.where(strict, a4[:LANES, :LANES], 0.0))
        a_kb = jnp.where(strict, a4[:LANES, LANES:], 0.0)
        ark[u] = bf(jnp.where(incl, a4[LANES:, :LANES], 0.0))
        arb[u] = bf(jnp.where(incl, a4[LANES:, LANES:], 0.0))
        akb[u] = bf(a_kb)
        xs[u] = eye - jnp.where(level(1), a_kb, 0.0)
    s = 2
    while s < seg:
        lvl_b = jnp.where(level(s), 1.0, 0.0).astype(BF16)
        xb = {u: bf(xs[u]) for u in units}
        t1 = {u: bf(_dot(xb[u], akb[u] * lvl_b)) for u in units}
        xs = {u: xs[u] - _dot(t1[u], xb[u]) for u in units}
        s *= 2
    xb = {u: bf(xs[u]) for u in units}
    vv = {u: twice(v_ref[sl[u]]) for u in units}

    lkf = {u: lkk_ref[sl[u]].astype(F32) for u in units}
    lrf = {u: lr_ref[sl[u]].astype(F32) for u in units}
    g_k = {(c, p): jnp.concatenate([_dot_nt(lkf[c, p][ss], st_ref[c * nseg + i, p])
                                    for i, ss in enumerate(segs)], 0) for c, p in units}
    g_r = {(c, p): jnp.concatenate([_dot_nt(lrf[c, p][ss], st_ref[c * nseg + i, p])
                                    for i, ss in enumerate(segs)], 0) for c, p in units}
    rhs = {u: bf(twice(g_k[u]) + _dot(akk[u], vv[u])) for u in units}
    sk = {u: unstack(_dot(xb[u], rhs[u])) for u in units}
    y2 = {u: _dot(ark[u], vv[u]) - _dot(arb[u], twice(bf(sk[u]))) for u in units}
    ys = {u: g_r[u] + unstack(y2[u]) for u in units}
    for c, p in units:
        rs, cs = sl[c, p]
        vf, ukf, ubf = v_ref[rs, cs].astype(F32), uk_ref[rs, cs].astype(F32), ub_ref[rs, cs].astype(F32)
        for i, ss in enumerate(segs):
            upd = _dot_tn(jnp.concatenate([vf[ss], sk[c, p][ss]], 0),
                          jnp.concatenate([ukf[ss], ubf[ss]], 0))
            n = c * nseg + i
            new = st_ref[n, p] * wc_ref[n, :, cs] + upd
            sout_ref[n, 2 * p] = new[:A_HEAD, :A_HEAD]
            sout_ref[n, 2 * p + 1] = new[A_HEAD:, A_HEAD:]

    for c in range(nck):
        rs = slice(c * c64, (c + 1) * c64)
        y = jnp.concatenate([ys[c, p] for p in range(A_PAIRS)], -1)
        mu = _half_sums(y) * (1.0 / A_HEAD)
        d = y - mu
        var = _half_sums(d * d) * (1.0 / A_HEAD)
        yn = d * lax.rsqrt(var + GN_EPS_A) * gng_ref[...] + gnb_ref[...]
        ya_ref[rs, :] = ((yn + bonus_ref[rs, :]) * g_ref[rs, :]).astype(BF16)


def _scan(pre, s0, layer, gn_g, gn_b, nb, nj, nck, nseg, carry):
    lkk, lr, rhk, rhb, uk, ub, v, wc, g, bonus = pre
    t = lkk.shape[0]
    rows = nck * WKV_CHUNK
    blk = pl.BlockSpec((rows, A_WIDTH), lambda i, j: (i * nj + j, 0))
    wcs = pl.BlockSpec((nck * nseg, 1, A_WIDTH), lambda i, j: (i * nj + j, 0, 0))
    vec = pl.BlockSpec((1, A_WIDTH), lambda i, j: (0, 0))
    if carry:
        units = nck * A_PAIRS
        body = functools.partial(_scan_seq_kernel, nck=nck)
        s_in = s_out = pl.BlockSpec((1, A_PAIRS, LANES, LANES), lambda i, j: (i, 0, 0, 0))
        s_shape = s0.shape
        scratch = [pltpu.VMEM((units, LANES, LANES), BF16), pltpu.VMEM((units, LANES, LANES), F32),
                   pltpu.VMEM((units, LANES, LANES), BF16), pltpu.VMEM((units, WKV_CHUNK, LANES), F32),
                   pltpu.VMEM((units, LANES, LANES), F32)]
    else:
        nst = nck * nseg
        body = functools.partial(_scan_multi_kernel, nck=nck, nseg=nseg)
        s_in = pl.BlockSpec((None, nst, A_HEADS, A_HEAD, A_HEAD), lambda i, j: (layer, i, 0, 0, 0))
        s_out = pl.BlockSpec((nst, A_HEADS, A_HEAD, A_HEAD), lambda i, j: (i, 0, 0, 0))
        s_shape = s0.shape[1:]
        scratch = [pltpu.VMEM((nst, A_PAIRS, LANES, LANES), F32)]
    return pl.pallas_call(
        body,
        grid=(nb, nj),
        in_specs=[blk] * 7 + [wcs, s_in, blk, blk, vec, vec],
        out_specs=[blk, s_out],
        scratch_shapes=scratch,
        out_shape=[jax.ShapeDtypeStruct((t, A_WIDTH), BF16), jax.ShapeDtypeStruct(s_shape, F32)],
        compiler_params=pltpu.CompilerParams(dimension_semantics=("parallel", "arbitrary"),
                                             vmem_limit_bytes=VMEM_LIMIT),
        name="rwkv_scan",
    )(lkk, lr, rhk, rhb, uk, ub, v, wc, s0, g, bonus, gn_g, gn_b)


def _ret_kernel(q_ref, k_ref, v_ref, gate_ref, cos_ref, sin_ref, din_ref, dq_ref, dk_ref, dc_ref,
                s0_ref, gng_ref, gnb_ref, yb_ref, st_ref, *, nck, nseg, carry):
    ch = RET_CHUNK
    seg = ch // nseg
    if carry:
        @pl.when(pl.program_id(1) == 0)
        def _():
            st_ref[...] = s0_ref[...]
    else:
        st_ref[...] = s0_ref[...]

    lane = lax.broadcasted_iota(jnp.int32, (ch, B_QK), 1)
    first = (lane % B_DK) < (B_DK // 2)
    lane_row = lax.broadcasted_iota(jnp.int32, (1, LANES), 1)
    masks = [jnp.where(lane_row < B_DK, 1.0, 0.0), jnp.where(lane_row < B_DK, 0.0, 1.0)]

    def rotary(x, cos, sin):
        swapped = jnp.where(first, pltpu.roll(x, B_QK - B_DK // 2, axis=1),
                            pltpu.roll(x, B_DK // 2, axis=1))
        return x * cos + swapped * sin

    for c in range(nck):
        rs = slice(c * ch, (c + 1) * ch)
        cos = jnp.concatenate([cos_ref[rs, :]] * (B_QK // LANES), -1)
        sin = jnp.concatenate([sin_ref[rs, :]] * (B_QK // LANES), -1)
        q = rotary(q_ref[rs, :].astype(F32), cos, sin)
        k = rotary(k_ref[rs, :].astype(F32), cos, sin) * (B_DK ** -0.5)
        for h in range(B_HEADS):
            ps = slice((h // 2) * LANES, (h // 2 + 1) * LANES)
            hs = slice(h * B_DV, (h + 1) * B_DV)
            qp = q[:, ps]
            km = k[:, ps] * masks[h % 2]
            vb = v_ref[rs, hs]
            scores = _dot_nt(qp.astype(BF16), km.astype(BF16)) * din_ref[h]
            o = _dot(scores.astype(BF16), vb)
            qd = qp * dq_ref[h]
            kd = km * dk_ref[h]
            dc = dc_ref[h][0:1, :]
            own = slice((h % 2) * B_DK, (h % 2 + 1) * B_DK)
            zero = jnp.zeros((B_DK, B_DV), F32)
            pad = lambda s_: jnp.concatenate([s_, zero] if h % 2 == 0 else [zero, s_], 0)
            if nseg == 1:
                st = st_ref[0, h]
                o = o + _dot(qd.astype(BF16), pad(st).astype(BF16))
                st_ref[0, h] = st * dc + _dot_tn(kd.astype(BF16), vb)[own]
            else:
                vf = v_ref[rs, hs].astype(F32)
                cross = []
                for s_ in range(nseg):
                    ss = slice(s_ * seg, (s_ + 1) * seg)
                    i = c * nseg + s_
                    st = st_ref[i, h]
                    cross.append(_dot(qd[ss], pad(st)))
                    st_ref[i, h] = st * dc + _dot_tn(kd[ss], vf[ss])[own]
                o = o + jnp.concatenate(cross, 0)
            mu = jnp.mean(o, -1, keepdims=True)
            d = o - mu
            var = jnp.mean(d * d, -1, keepdims=True)
            yn = d * lax.rsqrt(var + GN_EPS_B) * gng_ref[:, hs] + gnb_ref[:, hs]
            gate = gate_ref[rs, hs].astype(F32)
            yb_ref[rs, hs] = (yn * (gate * jax.nn.sigmoid(gate))).astype(BF16)


def _retention(zb, tabs, s0, layer, gn_g, gn_b, nb, nj, nck, nseg, carry, tab_rows_fixed):
    cos, sin, din, dq, dk, dc = tabs
    t = zb.shape[0]
    rows = nck * RET_CHUNK
    nst = s0.shape[1] // nb
    qs = pl.BlockSpec((rows, B_QK), lambda i, j: (i * nj + j, 0))
    ks = pl.BlockSpec((rows, B_QK), lambda i, j: (i * nj + j, 1))
    vs = pl.BlockSpec((rows, B_V), lambda i, j: (i * nj + j, 1))
    gs = pl.BlockSpec((rows, B_V), lambda i, j: (i * nj + j, 2))
    if tab_rows_fixed:
        tab = pl.BlockSpec((rows, LANES), lambda i, j: (0, 0))
    else:
        tab = pl.BlockSpec((rows, LANES), lambda i, j: (j, 0))
    s_in = pl.BlockSpec((None, nst, B_HEADS, B_DK, B_DV), lambda i, j: (layer, i, 0, 0, 0))
    sts = pl.BlockSpec((nst, B_HEADS, B_DK, B_DV), lambda i, j: (i, 0, 0, 0))
    const = lambda a: pl.BlockSpec(a.shape, lambda i, j: (0,) * a.ndim)
    return pl.pallas_call(
        functools.partial(_ret_kernel, nck=nck, nseg=nseg, carry=carry),
        grid=(nb, nj),
        in_specs=[qs, ks, vs, gs, tab, tab, const(din), const(dq), const(dk), const(dc), s_in,
                  const(gn_g), const(gn_b)],
        out_specs=[pl.BlockSpec((rows, B_V), lambda i, j: (i * nj + j, 0)), sts],
        out_shape=[jax.ShapeDtypeStruct((t, B_V), BF16), jax.ShapeDtypeStruct(s0.shape[1:], F32)],
        compiler_params=pltpu.CompilerParams(dimension_semantics=("parallel", "arbitrary"),
                                             vmem_limit_bytes=VMEM_LIMIT),
        name="retention",
    )(zb, zb, zb, zb, cos, sin, din, dq, dk, dc, s0, gn_g, gn_b)


def _merge_ffn_kernel(x_ref, ya_ref, yb_ref, zg_ref, woa_ref, wob_ref, wo_ref, gpost_ref, gpre_ref,
                      wgate_ref, wup_ref, wdown_ref, gfpost_ref, out_ref, *, nsub):
    rows = _sub_rows(x_ref.shape[0], nsub)
    sig = jax.nn.sigmoid
    pa = [_dot(ya_ref[r, :], woa_ref[...]) for r in rows]
    pb = [_dot(yb_ref[r, :], wob_ref[...]) for r in rows]
    m = [(sig(zg_ref[r, :D_MODEL].astype(F32)) * a + sig(zg_ref[r, D_MODEL:].astype(F32)) * b).astype(BF16)
         for r, a, b in zip(rows, pa, pb)]
    mo = [_dot(m_, wo_ref[...]) for m_ in m]
    x1 = [x_ref[r, :] + _rms(o, gpost_ref[...]) for r, o in zip(rows, mo)]
    hb = [_rms(x_, gpre_ref[...]).astype(BF16) for x_ in x1]
    gate = [_dot(h, wgate_ref[...]) for h in hb]
    up = [_dot(h, wup_ref[...]) for h in hb]
    act = [(g_ * sig(g_) * u).astype(BF16) for g_, u in zip(gate, up)]
    f = [_dot(a, wdown_ref[...]) for a in act]
    for r, x_, f_ in zip(rows, x1, f):
        out_ref[r, :] = x_ + _rms(f_, gfpost_ref[...])


def _merge_ffn(x, ya, yb, zg, mp, tm, nsub):
    t = x.shape[0]
    row = lambda n: pl.BlockSpec((tm, n), lambda i: (i, 0))
    ws = [mp['w_out_a'], mp['w_out_b'], mp['w_o'], mp['g_post'], mp['g_ffn_pre'], mp['w_gate'],
          mp['w_up'], mp['w_down'], mp['g_ffn_post']]
    return pl.pallas_call(
        functools.partial(_merge_ffn_kernel, nsub=nsub),
        grid=(t // tm,),
        in_specs=[row(D_MODEL), row(A_WIDTH), row(B_V), row(2 * D_MODEL)] +
                 [_resident(w.shape) for w in ws],
        out_specs=row(D_MODEL),
        out_shape=jax.ShapeDtypeStruct((t, D_MODEL), F32),
        compiler_params=pltpu.CompilerParams(dimension_semantics=("parallel",),
                                             vmem_limit_bytes=VMEM_LIMIT),
        name="merge_ffn",
    )(x, ya, yb, zg, *ws)


def _layer_params(l, w_in, mu_shift, w0, lora_w_up, a0, lora_a_up, lora_g_up, k_k, k_a, r_k):
    wi = w_in[l]
    row = lambda a: a.reshape(1, -1)
    wlora = jnp.zeros((LANES, 2 * A_WIDTH), F32)
    wlora = wlora.at[:LORA_W, :A_WIDTH].set(lora_w_up[l]).at[LORA_W:, A_WIDTH:].set(lora_a_up[l])
    lg = jnp.zeros((SHIFT_PAD - 3 * A_WIDTH - LANES, A_WIDTH), F32).at[:LORA_G].set(lora_g_up[l])
    return {
        'w_in': jnp.concatenate([wi[:, :SHIFT_W].astype(BF16),
                                 jnp.zeros((D_MODEL, SHIFT_PAD - SHIFT_W), BF16),
                                 wi[:, SHIFT_W:].astype(BF16)], 1),
        'mu': jnp.pad(row(mu_shift[l]), ((0, 0), (0, SHIFT_PAD - SHIFT_W))),
        'wlora': wlora.astype(BF16), 'lg': lg.astype(BF16),
        'w0': row(w0[l]), 'a0': row(a0[l]), 'k_k': row(k_k[l]), 'k_a': row(k_a[l]), 'r_k': row(r_k[l]),
    }


def _ret_tables(pos, seg, rows):
    half = B_DK // 2
    inv = ROPE_BASE ** (-jnp.arange(half, dtype=F32) / half)
    ang = pos.astype(F32)[:, None] * inv[None, :]
    cos, sin = jnp.cos(ang), jnp.sin(ang)
    cos = jnp.tile(jnp.concatenate([cos, cos], -1), (1, LANES // B_DK))
    sin = jnp.tile(jnp.concatenate([-sin, sin], -1), (1, LANES // B_DK))
    log_g = jnp.log(1.0 - jnp.exp2(-5.0 - jnp.arange(B_HEADS, dtype=F32)))
    idx = jnp.arange(rows)
    loc = (idx % seg).astype(F32)
    rel = loc[:, None] - loc[None, :]
    ok = ((idx[:, None] // seg) == (idx[None, :] // seg)) & (rel >= 0)
    din = jnp.where(ok[None], jnp.exp(log_g[:, None, None] * jnp.where(ok, rel, 0.0)[None]), 0.0)
    bc = lambda col: jnp.broadcast_to(col[:, :, None], (B_HEADS, rows, LANES))
    dq = bc(jnp.exp(log_g[:, None] * (loc[None, :] + 1.0)))
    dk = bc(jnp.exp(log_g[:, None] * (seg - 1.0 - loc[None, :])))
    dc = jnp.broadcast_to(jnp.exp(log_g * seg)[:, None, None], (B_HEADS, 8, LANES))
    return cos, sin, din, dq, dk, dc


def _extract_wkv(s):
    b = s.shape[0]
    return jnp.stack([s[:, :, :A_HEAD, :A_HEAD], s[:, :, A_HEAD:, A_HEAD:]], 2).reshape(
        b, A_HEADS, A_HEAD, A_HEAD)


def _group_layer(x, nseq, seq_len, pos0, states, layer, lp, mp, gn):
    t = x.shape[0]
    long_seq = seq_len >= RET_CHUNK
    tm, nsub = 512, 2
    za, zb, zg = _in_proj(x, lp['g_pre'], lp['w_in'], tm, nsub)

    if long_seq:
        wkv_nseg, ret_nseg = 1, 1
        nck_a, nck_b = 4, 2
        prep_rows = 512
        nb, nj_a, nj_b = nseq, seq_len // (nck_a * WKV_CHUNK), seq_len // (nck_b * RET_CHUNK)
        pos = pos0 + jnp.arange(seq_len)
        tabs = _ret_tables(pos, RET_CHUNK, RET_CHUNK)
    else:
        wkv_nseg, ret_nseg = WKV_CHUNK // seq_len, RET_CHUNK // seq_len
        nck_a, nck_b = 2, 1
        prep_rows = RET_CHUNK
        nb, nj_a, nj_b = t // (nck_a * WKV_CHUNK), 1, 1
        pos = pos0 + (jnp.arange(RET_CHUNK) % seq_len)
        tabs = _ret_tables(pos, seq_len, RET_CHUNK)
    nb_b = t // (nck_b * RET_CHUNK * nj_b)

    if states is None:
        s_wkv = jnp.zeros((nseq, A_PAIRS, LANES, LANES), F32)
        s_ret = jnp.zeros((1, nseq, B_HEADS, B_DK, B_DV), F32)
        shift_rows = None
        layer = 0
    else:
        s_wkv, s_ret = states[0], states[1]
        shift_rows = jnp.repeat(jnp.pad(states[2], ((0, 0), (0, SHIFT_PAD - SHIFT_W))), seq_len, axis=0)

    pre = _prep(za, shift_rows, lp, seq_len, min(seq_len, WKV_CHUNK), prep_rows)
    ya, wkv = _scan(pre, s_wkv, layer, gn['a_g'], gn['a_b'], nb, nj_a, nck_a, wkv_nseg, long_seq)
    yb, ret = _retention(zb, tabs, s_ret, layer, gn['b_g'], gn['b_b'], nb_b, nj_b, nck_b, ret_nseg,
                         long_seq, not long_seq)
    x = _merge_ffn(x, ya, yb, zg, mp, tm, nsub)
    shift = za.reshape(nseq, seq_len, SHIFT_PAD)[:, -1, :SHIFT_W]
    return x, (_extract_wkv(wkv) if long_seq else wkv), ret, shift


def kernel(x_prompt, x_sample, state_wkv, state_ret, state_shift, norm_mix_pre, w_in, mu_shift, w0,
           lora_w_up, a0, lora_a_up, lora_g_up, k_k, k_a, r_k, gn_a_gain, gn_a_bias, w_out_a,
           gn_b_gain, gn_b_bias, w_out_b, w_o, norm_mix_post, norm_ffn_pre, w_ffn_gate, w_ffn_up,
           w_ffn_down, norm_ffn_post):
    bp, lp_, _ = x_prompt.shape
    bs, ls_, _ = x_sample.shape
    depth = w_in.shape[0]
    yp = x_prompt.reshape(bp * lp_, D_MODEL)
    ys = x_sample.reshape(bs * ls_, D_MODEL)
    row = lambda a: a.reshape(1, -1)
    outs = [[] for _ in range(6)]
    for l in range(depth):
        lp = _layer_params(l, w_in, mu_shift, w0, lora_w_up, a0, lora_a_up, lora_g_up, k_k, k_a, r_k)
        lp['g_pre'] = row(norm_mix_pre[l])
        mp = {'w_out_a': w_out_a[l].astype(BF16), 'w_out_b': w_out_b[l].astype(BF16),
              'w_o': w_o[l].astype(BF16), 'g_post': row(norm_mix_post[l]),
              'g_ffn_pre': row(norm_ffn_pre[l]), 'w_gate': w_ffn_gate[l].astype(BF16),
              'w_up': w_ffn_up[l].astype(BF16), 'w_down': w_ffn_down[l].astype(BF16),
              'g_ffn_post': row(norm_ffn_post[l])}
        gn = {'a_g': row(gn_a_gain[l]), 'a_b': row(gn_a_bias[l]),
              'b_g': row(gn_b_gain[l]), 'b_b': row(gn_b_bias[l])}
        yp, a_, b_, c_ = _group_layer(yp, bp, lp_, 0, None, l, lp, mp, gn)
        outs[0].append(a_); outs[1].append(b_); outs[2].append(c_)
        ys, a_, b_, c_ = _group_layer(ys, bs, ls_, PAST_LEN,
                                      (state_wkv, state_ret, state_shift[l]), l, lp, mp, gn)
        outs[3].append(a_); outs[4].append(b_); outs[5].append(c_)
    return (yp.reshape(bp, lp_, D_MODEL), ys.reshape(bs, ls_, D_MODEL)) + \
        tuple(jnp.stack(o) for o in outs)
```

```python
import functools

import jax
import jax.numpy as jnp
from jax import lax
from jax.experimental import pallas as pl
from jax.experimental.pallas import tpu as pltpu

F32 = jnp.float32
BF16 = jnp.bfloat16

LANES = 128
D_MODEL = 1024
PAST_LEN = 16384
A_HEADS = 8
A_HEAD = 64
A_WIDTH = A_HEADS * A_HEAD
A_PAIRS = A_WIDTH // LANES
LORA_W = 64
LORA_A = 64
LORA_G = 160
SHIFT_W = 3 * A_WIDTH + LORA_W + LORA_A + LORA_G
SHIFT_PAD = 1920
GN_EPS_A = 64e-5
B_HEADS = 8
B_DK = 64
B_DV = 128
B_QK = B_HEADS * B_DK
B_V = B_HEADS * B_DV
RET_W = 2 * B_QK + 2 * B_V
RET_CHUNK = 128
ROPE_BASE = 10000.0
GN_EPS_B = 1e-5
D_FF = 2816
RMS_EPS = 1e-6
WKV_CHUNK = 64

VMEM_LIMIT = 56 * 1024 * 1024


def _resident(shape):
    nd = len(shape)
    return pl.BlockSpec(shape, lambda *_: (0,) * nd, pipeline_mode=pl.Buffered(1))


def _dot(a, b):
    return jnp.dot(a, b, preferred_element_type=F32)


def _dot_nt(a, b):
    return lax.dot_general(a, b, (((1,), (1,)), ((), ())), preferred_element_type=F32)


def _dot_tn(a, b):
    return lax.dot_general(a, b, (((0,), (0,)), ((), ())), preferred_element_type=F32)


def _rms(x, g):
    return x * lax.rsqrt(jnp.mean(x * x, -1, keepdims=True) + RMS_EPS) * g


def _half_sums(x):
    rows, width = x.shape
    lo = lax.broadcasted_iota(jnp.int32, (rows, LANES), 1) < A_HEAD
    out = []
    for p in range(width // LANES):
        xp = x[:, p * LANES:(p + 1) * LANES]
        s_lo = jnp.sum(jnp.where(lo, xp, 0.0), -1, keepdims=True)
        s_hi = jnp.sum(jnp.where(lo, 0.0, xp), -1, keepdims=True)
        out.append(jnp.where(lo, s_lo, s_hi))
    return jnp.concatenate(out, -1)


def _sub_rows(n, nsub):
    sub = n // nsub
    return [slice(i * sub, (i + 1) * sub) for i in range(nsub)]


def _in_proj_kernel(x_ref, g_ref, w_ref, za_ref, zb_ref, zg_ref, *, nsub):
    rows = _sub_rows(x_ref.shape[0], nsub)
    hb = [_rms(x_ref[r, :], g_ref[...]).astype(BF16) for r in rows]
    col0 = 0
    for out_ref in (za_ref, zb_ref, zg_ref):
        cols = slice(col0, col0 + out_ref.shape[1])
        col0 = cols.stop
        for r, h in zip(rows, hb):
            out_ref[r, :] = _dot(h, w_ref[:, cols]).astype(out_ref.dtype)


def _in_proj(x, g, w, tm, nsub):
    t = x.shape[0]
    row = lambda n: pl.BlockSpec((tm, n), lambda i: (i, 0))
    return pl.pallas_call(
        functools.partial(_in_proj_kernel, nsub=nsub),
        grid=(t // tm,),
        in_specs=[row(D_MODEL), _resident(g.shape), _resident(w.shape)],
        out_specs=[row(SHIFT_PAD), row(RET_W), row(2 * D_MODEL)],
        out_shape=[jax.ShapeDtypeStruct((t, SHIFT_PAD), F32), jax.ShapeDtypeStruct((t, RET_W), BF16),
                   jax.ShapeDtypeStruct((t, 2 * D_MODEL), BF16)],
        compiler_params=pltpu.CompilerParams(dimension_semantics=("parallel",),
                                             vmem_limit_bytes=VMEM_LIMIT),
        name="in_proj",
    )(x, g, w)


def _softplus(y):
    return jnp.maximum(y, 0.0) + jnp.log(1.0 + jnp.exp(-jnp.abs(y)))


def _prep_kernel(*refs, seq_len, seg, has_shift):
    if has_shift:
        za_ref, halo_ref, shift_ref = refs[:3]
        refs = refs[3:]
    else:
        za_ref, halo_ref = refs[:2]
        shift_ref = None
        refs = refs[2:]
    (mu_ref, wlora_ref, lg_ref, w0_ref, a0_ref, kk_ref, ka_ref, rk_ref,
     lkk_ref, lr_ref, rhk_ref, rhb_ref, uk_ref, ub_ref, v_ref, wc_ref, g_ref, bonus_ref) = refs

    za = za_ref[...]
    rows = za.shape[0]
    row = lax.broadcasted_iota(jnp.int32, (rows, 1), 0)
    prev = pltpu.roll(za, 1, axis=0)
    prev = jnp.where(row == 0, halo_ref[7:8, :], prev)
    if has_shift:
        prev = jnp.where(row % seq_len == 0, shift_ref[...], prev)
    else:
        starts = (pl.program_id(0) * rows) % seq_len == 0
        prev = jnp.where((row == 0) & starts, 0.0, prev)
    zs = za + mu_ref[...] * (prev - za)

    r = zs[:, 0:A_WIDTH]
    k = zs[:, A_WIDTH:2 * A_WIDTH]
    v = zs[:, 2 * A_WIDTH:3 * A_WIDTH]
    dwa = zs[:, 3 * A_WIDTH:3 * A_WIDTH + LANES]
    dg = zs[:, 3 * A_WIDTH + LANES:SHIFT_PAD]
    lane = lax.broadcasted_iota(jnp.int32, (rows, LANES), 1)
    act = jnp.where(lane < LORA_W, jnp.tanh(dwa), dwa).astype(BF16)
    lora = _dot(act, wlora_ref[...])
    w = -_softplus(-(w0_ref[...] + lora[:, :A_WIDTH])) - 0.5
    lw = -jnp.exp(w)
    a = jax.nn.sigmoid(a0_ref[...] + lora[:, A_WIDTH:])
    g_ref[...] = _dot(jax.nn.sigmoid(dg).astype(BF16), lg_ref[...])

    kk = k * kk_ref[...]
    kk = kk * lax.rsqrt(jnp.maximum(_half_sums(kk * kk), 1e-24))
    k2 = k * (1.0 + (a - 1.0) * ka_ref[...])
    b = kk * a
    bonus_ref[...] = _half_sums(r * k2 * rk_ref[...]) * v
    v_ref[...] = v.astype(BF16)

    ti = lax.broadcasted_iota(jnp.int32, (LANES, LANES), 0)
    tj = lax.broadcasted_iota(jnp.int32, (LANES, LANES), 1)
    same = (ti // seg) == (tj // seg)
    tri = jnp.concatenate([jnp.where(same & (tj <= ti), 1.0, 0.0),
                           jnp.where(same, 1.0, 0.0)], 0).astype(BF16)
    for g0 in range(0, rows, LANES):
        sl = slice(g0, g0 + LANES)
        x = lw[sl]
        hi = x.astype(BF16)
        r1 = x - hi.astype(F32)
        mid = r1.astype(BF16)
        lo = (r1 - mid.astype(F32)).astype(BF16)
        ct = _dot(tri, hi) + _dot(tri, mid) + _dot(tri, lo)
        cum, tot = ct[:LANES], ct[LANES:]
        inv = jnp.exp(-cum)
        tail = jnp.exp(tot - cum)
        lkk_ref[sl, :] = (kk[sl] * jnp.exp(cum - x)).astype(BF16)
        lr_ref[sl, :] = (r[sl] * jnp.exp(cum)).astype(BF16)
        rhk_ref[sl, :] = (k2[sl] * inv).astype(BF16)
        rhb_ref[sl, :] = (b[sl] * inv).astype(BF16)
        uk_ref[sl, :] = (k2[sl] * tail).astype(BF16)
        ub_ref[sl, :] = (-(b[sl] * tail)).astype(BF16)
        for s in range(LANES // seg):
            wc_ref[g0 // seg + s] = jnp.exp(tot[s * seg:s * seg + 1, :])


def _prep(za, shift_rows, pp, seq_len, seg, rows):
    t = za.shape[0]
    has_shift = shift_rows is not None
    blk = lambda n: pl.BlockSpec((rows, n), lambda i: (i, 0))
    halo = pl.BlockSpec((8, SHIFT_PAD), lambda i: (jnp.maximum(i * (rows // 8) - 1, 0), 0))
    params = [pp['mu'], pp['wlora'], pp['lg'], pp['w0'], pp['a0'], pp['k_k'], pp['k_a'], pp['r_k']]
    ins = [za, za] + ([shift_rows] if has_shift else []) + params
    in_specs = [blk(SHIFT_PAD), halo] + ([blk(SHIFT_PAD)] if has_shift else []) + \
        [_resident(p.shape) for p in params]
    bf = jax.ShapeDtypeStruct((t, A_WIDTH), BF16)
    f32 = jax.ShapeDtypeStruct((t, A_WIDTH), F32)
    out_shape = [bf] * 7 + [jax.ShapeDtypeStruct((t // seg, 1, A_WIDTH), F32), f32, f32]
    out_specs = [blk(A_WIDTH)] * 7 + [pl.BlockSpec((rows // seg, 1, A_WIDTH), lambda i: (i, 0, 0)),
                                      blk(A_WIDTH), blk(A_WIDTH)]
    return pl.pallas_call(
        functools.partial(_prep_kernel, seq_len=seq_len, seg=seg, has_shift=has_shift),
        grid=(t // rows,),
        in_specs=in_specs, out_specs=out_specs, out_shape=out_shape,
        compiler_params=pltpu.CompilerParams(dimension_semantics=("parallel",),
                                             vmem_limit_bytes=VMEM_LIMIT),
        name="rwkv_prep",
    )(*ins)


def _scan_seq_kernel(lkk_ref, lr_ref, rhk_ref, rhb_ref, uk_ref, ub_ref, v_ref, wc_ref, s0_ref,
                     g_ref, bonus_ref, gng_ref, gnb_ref, ya_ref, st_ref,
                     xlk_s, xavt_s, arb_s, yv_s, uv_s, *, nck):
    c64 = WKV_CHUNK

    @pl.when(pl.program_id(1) == 0)
    def _():
        st_ref[...] = s0_ref[...]

    ti = lax.broadcasted_iota(jnp.int32, (LANES, LANES), 0)
    tj = lax.broadcasted_iota(jnp.int32, (LANES, LANES), 1)
    same_head = (ti // c64) == (tj // c64)
    strict = same_head & (tj < ti)
    incl = same_head & (tj <= ti)
    eye = jnp.where(ti == tj, 1.0, 0.0)
    level = lambda s: ((ti // s) % 2 == 1) & ((tj // s) == (ti // s) - 1)
    levels = [2 ** e for e in range(1, 6)]
    level_b = {s: jnp.where(level(s), 1.0, 0.0).astype(BF16) for s in levels}
    lane_row = lax.broadcasted_iota(jnp.int32, (1, LANES), 1)
    m_lo = jnp.where(lane_row < A_HEAD, 1.0, 0.0).astype(BF16)
    m_hi = jnp.where(lane_row < A_HEAD, 0.0, 1.0).astype(BF16)
    lo_half = lax.broadcasted_iota(jnp.int32, (c64, LANES), 1) < A_HEAD
    stack_heads = lambda x: jnp.concatenate([x * m_lo, x * m_hi], 0)
    twice = lambda x: jnp.concatenate([x, x], 0)
    unstack = lambda x: jnp.where(lo_half, x[:c64], x[c64:])
    bf = lambda x: x.astype(BF16)

    units = [(c, p) for c in range(nck) for p in range(A_PAIRS)]
    group = 8
    for g0 in range(0, len(units), group):
        grp = [g0 + i for i in range(len(units[g0:g0 + group]))]
        sl = {u: (slice(units[u][0] * c64, (units[u][0] + 1) * c64),
                  slice(units[u][1] * LANES, (units[u][1] + 1) * LANES)) for u in grp}
        akk, akb, ark, xs = {}, {}, {}, {}
        for u in grp:
            rs, cs = sl[u]
            a4 = _dot_nt(jnp.concatenate([stack_heads(lkk_ref[rs, cs]), stack_heads(lr_ref[rs, cs])], 0),
                         jnp.concatenate([twice(rhk_ref[rs, cs]), twice(rhb_ref[rs, cs])], 0))
            akk[u] = bf(jnp.where(strict, a4[:LANES, :LANES], 0.0))
            a_kb = jnp.where(strict, a4[:LANES, LANES:], 0.0)
            ark[u] = bf(jnp.where(incl, a4[LANES:, :LANES], 0.0))
            arb_s[u] = bf(jnp.where(incl, a4[LANES:, LANES:], 0.0))
            akb[u] = bf(a_kb)
            xs[u] = eye - jnp.where(level(1), a_kb, 0.0)
        for s in levels:
            xb = {u: bf(xs[u]) for u in grp}
            t1 = {u: bf(_dot(xb[u], akb[u] * level_b[s])) for u in grp}
            xs = {u: xs[u] - _dot(t1[u], xb[u]) for u in grp}
        xb = {u: bf(xs[u]) for u in grp}
        vv = {u: twice(v_ref[sl[u]]) for u in grp}
        for u in grp:
            xlk_s[u] = bf(_dot(xb[u], twice(lkk_ref[sl[u]])))
        av = {u: bf(_dot(akk[u], vv[u])) for u in grp}
        for u in grp:
            xavt_s[u] = _dot(xb[u], av[u]).T
        for u in grp:
            yv_s[u] = unstack(_dot(ark[u], vv[u]))
        for u in grp:
            uv_s[u] = jnp.where(same_head, _dot_tn(v_ref[sl[u]], uk_ref[sl[u]]), 0.0)

    pairs = range(A_PAIRS)
    cols = [slice(p * LANES, (p + 1) * LANES) for p in pairs]
    for c in range(nck):
        rs = slice(c * c64, (c + 1) * c64)
        us = [c * A_PAIRS + p for p in pairs]
        st = [st_ref[0, p] for p in pairs]
        stb = [bf(s_) for s_ in st]
        skt = [bf(jnp.where(same_head, _dot_nt(stb[p], xlk_s[us[p]]) + xavt_s[us[p]], 0.0))
               for p in pairs]
        upd = [_dot(skt[p], stack_heads(ub_ref[rs, cols[p]])) for p in pairs]
        for p in pairs:
            st_ref[0, p] = st[p] * wc_ref[c, :, cols[p]] + uv_s[us[p]] + upd[p]
        g_r = [_dot_nt(lr_ref[rs, cols[p]], stb[p]) for p in pairs]
        y2 = [_dot_nt(arb_s[us[p]], skt[p]) for p in pairs]
        y = jnp.concatenate([g_r[p] + yv_s[us[p]] - (y2[p][:c64] + y2[p][c64:]) for p in pairs], -1)
        mu = _half_sums(y) * (1.0 / A_HEAD)
        d = y - mu
        var = _half_sums(d * d) * (1.0 / A_HEAD)
        yn = d * lax.rsqrt(var + GN_EPS_A) * gng_ref[...] + gnb_ref[...]
        ya_ref[rs, :] = ((yn + bonus_ref[rs, :]) * g_ref[rs, :]).astype(BF16)


def _scan_multi_kernel(lkk_ref, lr_ref, rhk_ref, rhb_ref, uk_ref, ub_ref, v_ref, wc_ref, s0_ref,
                       g_ref, bonus_ref, gng_ref, gnb_ref, ya_ref, sout_ref, st_ref, *, nck, nseg):
    c64 = WKV_CHUNK
    seg = c64 // nseg
    zero = jnp.zeros((A_HEAD, A_HEAD), F32)
    for n in range(nck * nseg):
        for p in range(A_PAIRS):
            st_ref[n, p] = jnp.concatenate([jnp.concatenate([s0_ref[n, 2 * p], zero], 1),
                                            jnp.concatenate([zero, s0_ref[n, 2 * p + 1]], 1)], 0)

    ti = lax.broadcasted_iota(jnp.int32, (LANES, LANES), 0)
    tj = lax.broadcasted_iota(jnp.int32, (LANES, LANES), 1)
    same_head = (ti // c64) == (tj // c64)
    same_seg = (ti // seg) == (tj // seg)
    strict = same_seg & (tj < ti)
    incl = same_seg & (tj <= ti)
    eye = jnp.where(ti == tj, 1.0, 0.0)
    lane_row = lax.broadcasted_iota(jnp.int32, (1, LANES), 1)
    m_lo = jnp.where(lane_row < A_HEAD, 1.0, 0.0).astype(BF16)
    m_hi = jnp.where(lane_row < A_HEAD, 0.0, 1.0).astype(BF16)
    lo_half = lax.broadcasted_iota(jnp.int32, (c64, LANES), 1) < A_HEAD
    stack_heads = lambda x: jnp.concatenate([x * m_lo, x * m_hi], 0)
    twice = lambda x: jnp.concatenate([x, x], 0)
    unstack = lambda x: jnp.where(lo_half, x[:c64], x[c64:])

    bf = lambda x: x.astype(BF16)
    level = lambda s: ((ti // s) % 2 == 1) & ((tj // s) == (ti // s) - 1)
    segs = [slice(s_ * seg, (s_ + 1) * seg) for s_ in range(nseg)]

    units = [(c, p) for c in range(nck) for p in range(A_PAIRS)]
    sl = {u: (slice(u[0] * c64, (u[0] + 1) * c64), slice(u[1] * LANES, (u[1] + 1) * LANES)) for u in units}
    akk, akb, ark, arb, xs = {}, {}, {}, {}, {}
    for u in units:
        rs, cs = sl[u]
        a4 = _dot_nt(jnp.concatenate([stack_heads(lkk_ref[rs, cs]), stack_heads(lr_ref[rs, cs])], 0),
                     jnp.concatenate([twice(rhk_ref[rs, cs]), twice(rhb_ref[rs, cs])], 0))
        akk[u] = bf(jnp.where(strict, a4[:LANES, :LANES], 0.0))
        a_kb = jnp.where(strict, a4[:LANES, LANES:], 0.0)
        ark[u] = bf(jnp.where(incl, a4[LANES:, :LANES], 0.0))
        arb[u] = bf(jnp.where(incl, a4[LANES:, LANES:], 0.0))
        akb[u] = bf(a_kb)
        xs[u] = eye - jnp.where(level(1), a_kb, 0.0)
    s = 2
    while s < seg:
        lvl_b = jnp.where(level(s), 1.0, 0.0).astype(BF16)
        xb = {u: bf(xs[u]) for u in units}
        t1 = {u: bf(_dot(xb[u], akb[u] * lvl_b)) for u in units}
        xs = {u: xs[u] - _dot(t1[u], xb[u]) for u in units}
        s *= 2
    xb = {u: bf(xs[u]) for u in units}
    vv = {u: twice(v_ref[sl[u]]) for u in units}

    lkf = {u: lkk_ref[sl[u]].astype(F32) for u in units}
    lrf = {u: lr_ref[sl[u]].astype(F32) for u in units}
    g_k = {(c, p): jnp.concatenate([_dot_nt(lkf[c, p][ss], st_ref[c * nseg + i, p])
                                    for i, ss in enumerate(segs)], 0) for c, p in units}
    g_r = {(c, p): jnp.concatenate([_dot_nt(lrf[c, p][ss], st_ref[c * nseg + i, p])
                                    for i, ss in enumerate(segs)], 0) for c, p in units}
    rhs = {u: bf(twice(g_k[u]) + _dot(akk[u], vv[u])) for u in units}
    sk = {u: unstack(_dot(xb[u], rhs[u])) for u in units}
    y2 = {u: _dot(ark[u], vv[u]) - _dot(arb[u], twice(bf(sk[u]))) for u in units}
    ys = {u: g_r[u] + unstack(y2[u]) for u in units}
    for c, p in units:
        rs, cs = sl[c, p]
        vf, ukf, ubf = v_ref[rs, cs].astype(F32), uk_ref[rs, cs].astype(F32), ub_ref[rs, cs].astype(F32)
        for i, ss in enumerate(segs):
            upd = _dot_tn(jnp.concatenate([vf[ss], sk[c, p][ss]], 0),
                          jnp.concatenate([ukf[ss], ubf[ss]], 0))
            n = c * nseg + i
            new = st_ref[n, p] * wc_ref[n, :, cs] + upd
            sout_ref[n, 2 * p] = new[:A_HEAD, :A_HEAD]
            sout_ref[n, 2 * p + 1] = new[A_HEAD:, A_HEAD:]

    for c in range(nck):
        rs = slice(c * c64, (c + 1) * c64)
        y = jnp.concatenate([ys[c, p] for p in range(A_PAIRS)], -1)
        mu = _half_sums(y) * (1.0 / A_HEAD)
        d = y - mu
        var = _half_sums(d * d) * (1.0 / A_HEAD)
        yn = d * lax.rsqrt(var + GN_EPS_A) * gng_ref[...] + gnb_ref[...]
        ya_ref[rs, :] = ((yn + bonus_ref[rs, :]) * g_ref[rs, :]).astype(BF16)


def _scan(pre, s0, layer, gn_g, gn_b, nb, nj, nck, nseg, carry):
    lkk, lr, rhk, rhb, uk, ub, v, wc, g, bonus = pre
    t = lkk.shape[0]
    rows = nck * WKV_CHUNK
    blk = pl.BlockSpec((rows, A_WIDTH), lambda i, j: (i * nj + j, 0))
    wcs = pl.BlockSpec((nck * nseg, 1, A_WIDTH), lambda i, j: (i * nj + j, 0, 0))
    vec = pl.BlockSpec((1, A_WIDTH), lambda i, j: (0, 0))
    if carry:
        units = nck * A_PAIRS
        body = functools.partial(_scan_seq_kernel, nck=nck)
        s_in = s_out = pl.BlockSpec((1, A_PAIRS, LANES, LANES), lambda i, j: (i, 0, 0, 0))
        s_shape = s0.shape
        scratch = [pltpu.VMEM((units, LANES, LANES), BF16), pltpu.VMEM((units, LANES, LANES), F32),
                   pltpu.VMEM((units, LANES, LANES), BF16), pltpu.VMEM((units, WKV_CHUNK, LANES), F32),
                   pltpu.VMEM((units, LANES, LANES), F32)]
    else:
        nst = nck * nseg
        body = functools.partial(_scan_multi_kernel, nck=nck, nseg=nseg)
        s_in = pl.BlockSpec((None, nst, A_HEADS, A_HEAD, A_HEAD), lambda i, j: (layer, i, 0, 0, 0))
        s_out = pl.BlockSpec((nst, A_HEADS, A_HEAD, A_HEAD), lambda i, j: (i, 0, 0, 0))
        s_shape = s0.shape[1:]
        scratch = [pltpu.VMEM((nst, A_PAIRS, LANES, LANES), F32)]
    return pl.pallas_call(
        body,
        grid=(nb, nj),
        in_specs=[blk] * 7 + [wcs, s_in, blk, blk, vec, vec],
        out_specs=[blk, s_out],
        scratch_shapes=scratch,
        out_shape=[jax.ShapeDtypeStruct((t, A_WIDTH), BF16), jax.ShapeDtypeStruct(s_shape, F32)],
        compiler_params=pltpu.CompilerParams(dimension_semantics=("parallel", "arbitrary"),
                                             vmem_limit_bytes=VMEM_LIMIT),
        name="rwkv_scan",
    )(lkk, lr, rhk, rhb, uk, ub, v, wc, s0, g, bonus, gn_g, gn_b)


def _ret_kernel(q_ref, k_ref, v_ref, gate_ref, cos_ref, sin_ref, din_ref, dq_ref, dk_ref, dc_ref,
                s0_ref, gng_ref, gnb_ref, yb_ref, st_ref, *, nck, nseg, carry):
    ch = RET_CHUNK
    seg = ch // nseg
    if carry:
        @pl.when(pl.program_id(1) == 0)
        def _():
            st_ref[...] = s0_ref[...]
    else:
        st_ref[...] = s0_ref[...]

    lane = lax.broadcasted_iota(jnp.int32, (ch, B_QK), 1)
    first = (lane % B_DK) < (B_DK // 2)
    lane_row = lax.broadcasted_iota(jnp.int32, (1, LANES), 1)
    masks = [jnp.where(lane_row < B_DK, 1.0, 0.0), jnp.where(lane_row < B_DK, 0.0, 1.0)]

    def rotary(x, cos, sin):
        swapped = jnp.where(first, pltpu.roll(x, B_QK - B_DK // 2, axis=1),
                            pltpu.roll(x, B_DK // 2, axis=1))
        return x * cos + swapped * sin

    heads = range(B_HEADS)
    pcol = [slice((h // 2) * LANES, (h // 2 + 1) * LANES) for h in heads]
    hcol = [slice(h * B_DV, (h + 1) * B_DV) for h in heads]
    own = [slice((h % 2) * B_DK, (h % 2 + 1) * B_DK) for h in heads]
    dc = [dc_ref[h][0:1, :] for h in heads]
    zero = jnp.zeros((B_DK, B_DV), F32)
    pad = lambda h, s_: jnp.concatenate([s_, zero] if h % 2 == 0 else [zero, s_], 0)
    segs = [slice(s_ * seg, (s_ + 1) * seg) for s_ in range(nseg)]
    bf = lambda x: x.astype(BF16)
    if nseg == 1:
        st = [st_ref[0, h] for h in heads]

    for c in range(nck):
        rs = slice(c * ch, (c + 1) * ch)
        cos = jnp.concatenate([cos_ref[rs, :]] * (B_QK // LANES), -1)
        sin = jnp.concatenate([sin_ref[rs, :]] * (B_QK // LANES), -1)
        q = rotary(q_ref[rs, :].astype(F32), cos, sin)
        k = rotary(k_ref[rs, :].astype(F32), cos, sin) * (B_DK ** -0.5)
        qp = [q[:, pcol[h]] for h in heads]
        km = [k[:, pcol[h]] * masks[h % 2] for h in heads]
        vb = [v_ref[rs, hcol[h]] for h in heads]
        scores = [bf(_dot_nt(bf(qp[h]), bf(km[h])) * din_ref[h]) for h in heads]
        o = [_dot(scores[h], vb[h]) for h in heads]
        qd = [qp[h] * dq_ref[h] for h in heads]
        kd = [km[h] * dk_ref[h] for h in heads]
        if nseg == 1:
            o = [o[h] + _dot(bf(qd[h]), bf(pad(h, st[h]))) for h in heads]
            st = [st[h] * dc[h] + _dot_tn(bf(kd[h]), vb[h])[own[h]] for h in heads]
        else:
            o = [o[h] + jnp.concatenate([_dot(qd[h][ss], pad(h, st_ref[c * nseg + i, h]))
                                         for i, ss in enumerate(segs)], 0) for h in heads]
            for h in heads:
                vf = vb[h].astype(F32)
                for i, ss in enumerate(segs):
                    n = c * nseg + i
                    st_ref[n, h] = st_ref[n, h] * dc[h] + _dot_tn(kd[h][ss], vf[ss])[own[h]]
        mu = [jnp.mean(o[h], -1, keepdims=True) for h in heads]
        d = [o[h] - mu[h] for h in heads]
        var = [jnp.mean(d[h] * d[h], -1, keepdims=True) for h in heads]
        for h in heads:
            yn = d[h] * lax.rsqrt(var[h] + GN_EPS_B) * gng_ref[:, hcol[h]] + gnb_ref[:, hcol[h]]
            gate = gate_ref[rs, hcol[h]].astype(F32)
            yb_ref[rs, hcol[h]] = (yn * (gate * jax.nn.sigmoid(gate))).astype(BF16)
    if nseg == 1:
        for h in heads:
            st_ref[0, h] = st[h]


def _retention(zb, tabs, s0, layer, gn_g, gn_b, nb, nj, nck, nseg, carry, tab_rows_fixed):
    cos, sin, din, dq, dk, dc = tabs
    t = zb.shape[0]
    rows = nck * RET_CHUNK
    nst = s0.shape[1] // nb
    qs = pl.BlockSpec((rows, B_QK), lambda i, j: (i * nj + j, 0))
    ks = pl.BlockSpec((rows, B_QK), lambda i, j: (i * nj + j, 1))
    vs = pl.BlockSpec((rows, B_V), lambda i, j: (i * nj + j, 1))
    gs = pl.BlockSpec((rows, B_V), lambda i, j: (i * nj + j, 2))
    if tab_rows_fixed:
        tab = pl.BlockSpec((rows, LANES), lambda i, j: (0, 0))
    else:
        tab = pl.BlockSpec((rows, LANES), lambda i, j: (j, 0))
    s_in = pl.BlockSpec((None, nst, B_HEADS, B_DK, B_DV), lambda i, j: (layer, i, 0, 0, 0))
    sts = pl.BlockSpec((nst, B_HEADS, B_DK, B_DV), lambda i, j: (i, 0, 0, 0))
    const = lambda a: pl.BlockSpec(a.shape, lambda i, j: (0,) * a.ndim)
    return pl.pallas_call(
        functools.partial(_ret_kernel, nck=nck, nseg=nseg, carry=carry),
        grid=(nb, nj),
        in_specs=[qs, ks, vs, gs, tab, tab, const(din), const(dq), const(dk), const(dc), s_in,
                  const(gn_g), const(gn_b)],
        out_specs=[pl.BlockSpec((rows, B_V), lambda i, j: (i * nj + j, 0)), sts],
        out_shape=[jax.ShapeDtypeStruct((t, B_V), BF16), jax.ShapeDtypeStruct(s0.shape[1:], F32)],
        compiler_params=pltpu.CompilerParams(dimension_semantics=("parallel", "arbitrary"),
                                             vmem_limit_bytes=VMEM_LIMIT),
        name="retention",
    )(zb, zb, zb, zb, cos, sin, din, dq, dk, dc, s0, gn_g, gn_b)


def _merge_ffn_kernel(x_ref, ya_ref, yb_ref, zg_ref, woa_ref, wob_ref, wo_ref, gpost_ref, gpre_ref,
                      wgate_ref, wup_ref, wdown_ref, gfpost_ref, out_ref, *, nsub):
    rows = _sub_rows(x_ref.shape[0], nsub)
    sig = jax.nn.sigmoid
    pa = [_dot(ya_ref[r, :], woa_ref[...]) for r in rows]
    pb = [_dot(yb_ref[r, :], wob_ref[...]) for r in rows]
    m = [(sig(zg_ref[r, :D_MODEL].astype(F32)) * a + sig(zg_ref[r, D_MODEL:].astype(F32)) * b).astype(BF16)
         for r, a, b in zip(rows, pa, pb)]
    mo = [_dot(m_, wo_ref[...]) for m_ in m]
    x1 = [x_ref[r, :] + _rms(o, gpost_ref[...]) for r, o in zip(rows, mo)]
    hb = [_rms(x_, gpre_ref[...]).astype(BF16) for x_ in x1]
    gate = [_dot(h, wgate_ref[...]) for h in hb]
    up = [_dot(h, wup_ref[...]) for h in hb]
    act = [(g_ * sig(g_) * u).astype(BF16) for g_, u in zip(gate, up)]
    f = [_dot(a, wdown_ref[...]) for a in act]
    for r, x_, f_ in zip(rows, x1, f):
        out_ref[r, :] = x_ + _rms(f_, gfpost_ref[...])


def _merge_ffn(x, ya, yb, zg, mp, tm, nsub):
    t = x.shape[0]
    row = lambda n: pl.BlockSpec((tm, n), lambda i: (i, 0))
    ws = [mp['w_out_a'], mp['w_out_b'], mp['w_o'], mp['g_post'], mp['g_ffn_pre'], mp['w_gate'],
          mp['w_up'], mp['w_down'], mp['g_ffn_post']]
    return pl.pallas_call(
        functools.partial(_merge_ffn_kernel, nsub=nsub),
        grid=(t // tm,),
        in_specs=[row(D_MODEL), row(A_WIDTH), row(B_V), row(2 * D_MODEL)] +
                 [_resident(w.shape) for w in ws],
        out_specs=row(D_MODEL),
        out_shape=jax.ShapeDtypeStruct((t, D_MODEL), F32),
        compiler_params=pltpu.CompilerParams(dimension_semantics=("parallel",),
                                             vmem_limit_bytes=VMEM_LIMIT),
        name="merge_ffn",
    )(x, ya, yb, zg, *ws)


def _layer_params(l, w_in, mu_shift, w0, lora_w_up, a0, lora_a_up, lora_g_up, k_k, k_a, r_k):
    wi = w_in[l]
    row = lambda a: a.reshape(1, -1)
    wlora = jnp.zeros((LANES, 2 * A_WIDTH), F32)
    wlora = wlora.at[:LORA_W, :A_WIDTH].set(lora_w_up[l]).at[LORA_W:, A_WIDTH:].set(lora_a_up[l])
    lg = jnp.zeros((SHIFT_PAD - 3 * A_WIDTH - LANES, A_WIDTH), F32).at[:LORA_G].set(lora_g_up[l])
    return {
        'w_in': jnp.concatenate([wi[:, :SHIFT_W].astype(BF16),
                                 jnp.zeros((D_MODEL, SHIFT_PAD - SHIFT_W), BF16),
                                 wi[:, SHIFT_W:].astype(BF16)], 1),
        'mu': jnp.pad(row(mu_shift[l]), ((0, 0), (0, SHIFT_PAD - SHIFT_W))),
        'wlora': wlora.astype(BF16), 'lg': lg.astype(BF16),
        'w0': row(w0[l]), 'a0': row(a0[l]), 'k_k': row(k_k[l]), 'k_a': row(k_a[l]), 'r_k': row(r_k[l]),
    }


def _ret_tables(pos, seg, rows):
    half = B_DK // 2
    inv = ROPE_BASE ** (-jnp.arange(half, dtype=F32) / half)
    ang = pos.astype(F32)[:, None] * inv[None, :]
    cos, sin = jnp.cos(ang), jnp.sin(ang)
    cos = jnp.tile(jnp.concatenate([cos, cos], -1), (1, LANES // B_DK))
    sin = jnp.tile(jnp.concatenate([-sin, sin], -1), (1, LANES // B_DK))
    log_g = jnp.log(1.0 - jnp.exp2(-5.0 - jnp.arange(B_HEADS, dtype=F32)))
    idx = jnp.arange(rows)
    loc = (idx % seg).astype(F32)
    rel = loc[:, None] - loc[None, :]
    ok = ((idx[:, None] // seg) == (idx[None, :] // seg)) & (rel >= 0)
    din = jnp.where(ok[None], jnp.exp(log_g[:, None, None] * jnp.where(ok, rel, 0.0)[None]), 0.0)
    bc = lambda col: jnp.broadcast_to(col[:, :, None], (B_HEADS, rows, LANES))
    dq = bc(jnp.exp(log_g[:, None] * (loc[None, :] + 1.0)))
    dk = bc(jnp.exp(log_g[:, None] * (seg - 1.0 - loc[None, :])))
    dc = jnp.broadcast_to(jnp.exp(log_g * seg)[:, None, None], (B_HEADS, 8, LANES))
    return cos, sin, din, dq, dk, dc


def _extract_wkv(s):
    b = s.shape[0]
    return jnp.stack([s[:, :, :A_HEAD, :A_HEAD], s[:, :, A_HEAD:, A_HEAD:]], 2).reshape(
        b, A_HEADS, A_HEAD, A_HEAD)


def _group_layer(x, nseq, seq_len, pos0, states, layer, lp, mp, gn):
    t = x.shape[0]
    long_seq = seq_len >= RET_CHUNK
    tm, nsub = 512, 2
    za, zb, zg = _in_proj(x, lp['g_pre'], lp['w_in'], tm, nsub)

    if long_seq:
        wkv_nseg, ret_nseg = 1, 1
        nck_a, nck_b = 4, 2
        prep_rows = 512
        nb, nj_a, nj_b = nseq, seq_len // (nck_a * WKV_CHUNK), seq_len // (nck_b * RET_CHUNK)
        pos = pos0 + jnp.arange(seq_len)
        tabs = _ret_tables(pos, RET_CHUNK, RET_CHUNK)
    else:
        wkv_nseg, ret_nseg = WKV_CHUNK // seq_len, RET_CHUNK // seq_len
        nck_a, nck_b = 2, 1
        prep_rows = RET_CHUNK
        nb, nj_a, nj_b = t // (nck_a * WKV_CHUNK), 1, 1
        pos = pos0 + (jnp.arange(RET_CHUNK) % seq_len)
        tabs = _ret_tables(pos, seq_len, RET_CHUNK)
    nb_b = t // (nck_b * RET_CHUNK * nj_b)

    if states is None:
        s_wkv = jnp.zeros((nseq, A_PAIRS, LANES, LANES), F32)
        s_ret = jnp.zeros((1, nseq, B_HEADS, B_DK, B_DV), F32)
        shift_rows = None
        layer = 0
    else:
        s_wkv, s_ret = states[0], states[1]
        shift_rows = jnp.repeat(jnp.pad(states[2], ((0, 0), (0, SHIFT_PAD - SHIFT_W))), seq_len, axis=0)

    pre = _prep(za, shift_rows, lp, seq_len, min(seq_len, WKV_CHUNK), prep_rows)
    ya, wkv = _scan(pre, s_wkv, layer, gn['a_g'], gn['a_b'], nb, nj_a, nck_a, wkv_nseg, long_seq)
    yb, ret = _retention(zb, tabs, s_ret, layer, gn['b_g'], gn['b_b'], nb_b, nj_b, nck_b, ret_nseg,
                         long_seq, not long_seq)
    x = _merge_ffn(x, ya, yb, zg, mp, tm, nsub)
    shift = za.reshape(nseq, seq_len, SHIFT_PAD)[:, -1, :SHIFT_W]
    return x, (_extract_wkv(wkv) if long_seq else wkv), ret, shift


def kernel(x_prompt, x_sample, state_wkv, state_ret, state_shift, norm_mix_pre, w_in, mu_shift, w0,
           lora_w_up, a0, lora_a_up, lora_g_up, k_k, k_a, r_k, gn_a_gain, gn_a_bias, w_out_a,
           gn_b_gain, gn_b_bias, w_out_b, w_o, norm_mix_post, norm_ffn_pre, w_ffn_gate, w_ffn_up,
           w_ffn_down, norm_ffn_post):
    bp, lp_, _ = x_prompt.shape
    bs, ls_, _ = x_sample.shape
    depth = w_in.shape[0]
    yp = x_prompt.reshape(bp * lp_, D_MODEL)
    ys = x_sample.reshape(bs * ls_, D_MODEL)
    row = lambda a: a.reshape(1, -1)
    outs = [[] for _ in range(6)]
    for l in range(depth):
        lp = _layer_params(l, w_in, mu_shift, w0, lora_w_up, a0, lora_a_up, lora_g_up, k_k, k_a, r_k)
        lp['g_pre'] = row(norm_mix_pre[l])
        mp = {'w_out_a': w_out_a[l].astype(BF16), 'w_out_b': w_out_b[l].astype(BF16),
              'w_o': w_o[l].astype(BF16), 'g_post': row(norm_mix_post[l]),
              'g_ffn_pre': row(norm_ffn_pre[l]), 'w_gate': w_ffn_gate[l].astype(BF16),
              'w_up': w_ffn_up[l].astype(BF16), 'w_down': w_ffn_down[l].astype(BF16),
              'g_ffn_post': row(norm_ffn_post[l])}
        gn = {'a_g': row(gn_a_gain[l]), 'a_b': row(gn_a_bias[l]),
              'b_g': row(gn_b_gain[l]), 'b_b': row(gn_b_bias[l])}
        yp, a_, b_, c_ = _group_layer(yp, bp, lp_, 0, None, l, lp, mp, gn)
        outs[0].append(a_); outs[1].append(b_); outs[2].append(c_)
        ys, a_, b_, c_ = _group_layer(ys, bs, ls_, PAST_LEN,
                                      (state_wkv, state_ret, state_shift[l]), l, lp, mp, gn)
        outs[3].append(a_); outs[4].append(b_); outs[5].append(c_)
    return (yp.reshape(bp, lp_, D_MODEL), ys.reshape(bs, ls_, D_MODEL)) + \
        tuple(jnp.stack(o) for o in outs)
```

```python
import functools

import jax
import jax.numpy as jnp
from jax import lax
from jax.experimental import pallas as pl
from jax.experimental.pallas import tpu as pltpu

F32 = jnp.float32
BF16 = jnp.bfloat16

LANES = 128
D_MODEL = 1024
PAST_LEN = 16384
A_HEADS = 8
A_HEAD = 64
A_WIDTH = A_HEADS * A_HEAD
A_PAIRS = A_WIDTH // LANES
LORA_W = 64
LORA_A = 64
LORA_G = 160
SHIFT_W = 3 * A_WIDTH + LORA_W + LORA_A + LORA_G
SHIFT_PAD = 1920
GN_EPS_A = 64e-5
B_HEADS = 8
B_DK = 64
B_DV = 128
B_QK = B_HEADS * B_DK
B_V = B_HEADS * B_DV
RET_W = 2 * B_QK + 2 * B_V
RET_CHUNK = 128
ROPE_BASE = 10000.0
GN_EPS_B = 1e-5
D_FF = 2816
RMS_EPS = 1e-6
WKV_CHUNK = 64

VMEM_LIMIT = 56 * 1024 * 1024


def _resident(shape):
    nd = len(shape)
    return pl.BlockSpec(shape, lambda *_: (0,) * nd, pipeline_mode=pl.Buffered(1))


def _resident_layer(a, layer):
    nd = a.ndim
    return pl.BlockSpec((None,) + a.shape[1:], lambda *_: (layer,) + (0,) * (nd - 1),
                        pipeline_mode=pl.Buffered(1))


def _dot(a, b):
    return jnp.dot(a, b, preferred_element_type=F32)


def _dot_nt(a, b):
    return lax.dot_general(a, b, (((1,), (1,)), ((), ())), preferred_element_type=F32)


def _dot_tn(a, b):
    return lax.dot_general(a, b, (((0,), (0,)), ((), ())), preferred_element_type=F32)


def _rms(x, g):
    return x * lax.rsqrt(jnp.mean(x * x, -1, keepdims=True) + RMS_EPS) * g


def _half_sums(x):
    rows, width = x.shape
    lo = lax.broadcasted_iota(jnp.int32, (rows, LANES), 1) < A_HEAD
    out = []
    for p in range(width // LANES):
        xp = x[:, p * LANES:(p + 1) * LANES]
        s_lo = jnp.sum(jnp.where(lo, xp, 0.0), -1, keepdims=True)
        s_hi = jnp.sum(jnp.where(lo, 0.0, xp), -1, keepdims=True)
        out.append(jnp.where(lo, s_lo, s_hi))
    return jnp.concatenate(out, -1)


def _sub_rows(n, nsub):
    sub = n // nsub
    return [slice(i * sub, (i + 1) * sub) for i in range(nsub)]


def _in_proj_kernel(x_ref, g_ref, w_ref, za_ref, zb_ref, zg_ref, *, nsub):
    rows = _sub_rows(x_ref.shape[0], nsub)
    hb = [_rms(x_ref[r, :], g_ref[...]).astype(BF16) for r in rows]
    col0 = 0
    for out_ref in (za_ref, zb_ref, zg_ref):
        cols = slice(col0, col0 + out_ref.shape[1])
        col0 = cols.stop
        for r, h in zip(rows, hb):
            out_ref[r, :] = _dot(h, w_ref[:, cols]).astype(out_ref.dtype)


def _in_proj(x, g, w, layer, tm, nsub):
    t = x.shape[0]
    row = lambda n: pl.BlockSpec((tm, n), lambda i: (i, 0))
    return pl.pallas_call(
        functools.partial(_in_proj_kernel, nsub=nsub),
        grid=(t // tm,),
        in_specs=[row(D_MODEL), _resident(g.shape), _resident_layer(w, layer)],
        out_specs=[row(SHIFT_PAD), row(RET_W), row(2 * D_MODEL)],
        out_shape=[jax.ShapeDtypeStruct((t, SHIFT_PAD), F32), jax.ShapeDtypeStruct((t, RET_W), BF16),
                   jax.ShapeDtypeStruct((t, 2 * D_MODEL), BF16)],
        compiler_params=pltpu.CompilerParams(dimension_semantics=("parallel",),
                                             vmem_limit_bytes=VMEM_LIMIT),
        name="in_proj",
    )(x, g, w)


def _softplus(y):
    return jnp.maximum(y, 0.0) + jnp.log(1.0 + jnp.exp(-jnp.abs(y)))


def _prep_kernel(*refs, seq_len, seg, has_shift):
    if has_shift:
        za_ref, halo_ref, shift_ref = refs[:3]
        refs = refs[3:]
    else:
        za_ref, halo_ref = refs[:2]
        shift_ref = None
        refs = refs[2:]
    (mu_ref, wlora_ref, lg_ref, w0_ref, a0_ref, kk_ref, ka_ref, rk_ref,
     lkk_ref, lr_ref, rhk_ref, rhb_ref, uk_ref, ub_ref, v_ref, wc_ref, g_ref, bonus_ref) = refs

    za = za_ref[...]
    rows = za.shape[0]
    row = lax.broadcasted_iota(jnp.int32, (rows, 1), 0)
    prev = pltpu.roll(za, 1, axis=0)
    prev = jnp.where(row == 0, halo_ref[7:8, :], prev)
    if has_shift:
        prev = jnp.where(row % seq_len == 0, shift_ref[...], prev)
    else:
        starts = (pl.program_id(0) * rows) % seq_len == 0
        prev = jnp.where((row == 0) & starts, 0.0, prev)
    zs = za + mu_ref[...] * (prev - za)

    r = zs[:, 0:A_WIDTH]
    k = zs[:, A_WIDTH:2 * A_WIDTH]
    v = zs[:, 2 * A_WIDTH:3 * A_WIDTH]
    dwa = zs[:, 3 * A_WIDTH:3 * A_WIDTH + LANES]
    dg = zs[:, 3 * A_WIDTH + LANES:SHIFT_PAD]
    lane = lax.broadcasted_iota(jnp.int32, (rows, LANES), 1)
    act = jnp.where(lane < LORA_W, jnp.tanh(dwa), dwa).astype(BF16)
    lora = _dot(act, wlora_ref[...])
    w = -_softplus(-(w0_ref[...] + lora[:, :A_WIDTH])) - 0.5
    lw = -jnp.exp(w)
    a = jax.nn.sigmoid(a0_ref[...] + lora[:, A_WIDTH:])
    g_ref[...] = _dot(jax.nn.sigmoid(dg).astype(BF16), lg_ref[...])

    kk = k * kk_ref[...]
    kk = kk * lax.rsqrt(jnp.maximum(_half_sums(kk * kk), 1e-24))
    k2 = k * (1.0 + (a - 1.0) * ka_ref[...])
    b = kk * a
    bonus_ref[...] = _half_sums(r * k2 * rk_ref[...]) * v
    v_ref[...] = v.astype(BF16)

    ti = lax.broadcasted_iota(jnp.int32, (LANES, LANES), 0)
    tj = lax.broadcasted_iota(jnp.int32, (LANES, LANES), 1)
    same = (ti // seg) == (tj // seg)
    tri = jnp.concatenate([jnp.where(same & (tj <= ti), 1.0, 0.0),
                           jnp.where(same, 1.0, 0.0)], 0).astype(BF16)
    for g0 in range(0, rows, LANES):
        sl = slice(g0, g0 + LANES)
        x = lw[sl]
        hi = x.astype(BF16)
        r1 = x - hi.astype(F32)
        mid = r1.astype(BF16)
        lo = (r1 - mid.astype(F32)).astype(BF16)
        ct = _dot(tri, hi) + _dot(tri, mid) + _dot(tri, lo)
        cum, tot = ct[:LANES], ct[LANES:]
        inv = jnp.exp(-cum)
        tail = jnp.exp(tot - cum)
        lkk_ref[sl, :] = (kk[sl] * jnp.exp(cum - x)).astype(BF16)
        lr_ref[sl, :] = (r[sl] * jnp.exp(cum)).astype(BF16)
        rhk_ref[sl, :] = (k2[sl] * inv).astype(BF16)
        rhb_ref[sl, :] = (b[sl] * inv).astype(BF16)
        uk_ref[sl, :] = (k2[sl] * tail).astype(BF16)
        ub_ref[sl, :] = (-(b[sl] * tail)).astype(BF16)
        for s in range(LANES // seg):
            wc_ref[g0 // seg + s] = jnp.exp(tot[s * seg:s * seg + 1, :])


def _prep(za, shift_rows, pp, seq_len, seg, rows):
    t = za.shape[0]
    has_shift = shift_rows is not None
    blk = lambda n: pl.BlockSpec((rows, n), lambda i: (i, 0))
    halo = pl.BlockSpec((8, SHIFT_PAD), lambda i: (jnp.maximum(i * (rows // 8) - 1, 0), 0))
    params = [pp['mu'], pp['wlora'], pp['lg'], pp['w0'], pp['a0'], pp['k_k'], pp['k_a'], pp['r_k']]
    ins = [za, za] + ([shift_rows] if has_shift else []) + params
    in_specs = [blk(SHIFT_PAD), halo] + ([blk(SHIFT_PAD)] if has_shift else []) + \
        [_resident(p.shape) for p in params]
    bf = jax.ShapeDtypeStruct((t, A_WIDTH), BF16)
    f32 = jax.ShapeDtypeStruct((t, A_WIDTH), F32)
    out_shape = [bf] * 7 + [jax.ShapeDtypeStruct((t // seg, 1, A_WIDTH), F32), f32, f32]
    out_specs = [blk(A_WIDTH)] * 7 + [pl.BlockSpec((rows // seg, 1, A_WIDTH), lambda i: (i, 0, 0)),
                                      blk(A_WIDTH), blk(A_WIDTH)]
    return pl.pallas_call(
        functools.partial(_prep_kernel, seq_len=seq_len, seg=seg, has_shift=has_shift),
        grid=(t // rows,),
        in_specs=in_specs, out_specs=out_specs, out_shape=out_shape,
        compiler_params=pltpu.CompilerParams(dimension_semantics=("parallel",),
                                             vmem_limit_bytes=VMEM_LIMIT),
        name="rwkv_prep",
    )(*ins)


def _scan_seq_kernel(lkk_ref, lr_ref, rhk_ref, rhb_ref, uk_ref, ub_ref, v_ref, wc_ref, s0_ref,
                     g_ref, bonus_ref, gng_ref, gnb_ref, ya_ref, st_ref,
                     xlk_s, xavt_s, arb_s, yv_s, uv_s, *, nck):
    c64 = WKV_CHUNK

    @pl.when(pl.program_id(1) == 0)
    def _():
        st_ref[...] = s0_ref[...]

    ti = lax.broadcasted_iota(jnp.int32, (LANES, LANES), 0)
    tj = lax.broadcasted_iota(jnp.int32, (LANES, LANES), 1)
    same_head = (ti // c64) == (tj // c64)
    strict = same_head & (tj < ti)
    incl = same_head & (tj <= ti)
    eye = jnp.where(ti == tj, 1.0, 0.0)
    level = lambda s: ((ti // s) % 2 == 1) & ((tj // s) == (ti // s) - 1)
    levels = [2 ** e for e in range(1, 6)]
    level_b = {s: jnp.where(level(s), 1.0, 0.0).astype(BF16) for s in levels}
    lane_row = lax.broadcasted_iota(jnp.int32, (1, LANES), 1)
    m_lo = jnp.where(lane_row < A_HEAD, 1.0, 0.0).astype(BF16)
    m_hi = jnp.where(lane_row < A_HEAD, 0.0, 1.0).astype(BF16)
    lo_half = lax.broadcasted_iota(jnp.int32, (c64, LANES), 1) < A_HEAD
    stack_heads = lambda x: jnp.concatenate([x * m_lo, x * m_hi], 0)
    twice = lambda x: jnp.concatenate([x, x], 0)
    unstack = lambda x: jnp.where(lo_half, x[:c64], x[c64:])
    bf = lambda x: x.astype(BF16)

    units = [(c, p) for c in range(nck) for p in range(A_PAIRS)]
    group = 8
    for g0 in range(0, len(units), group):
        grp = [g0 + i for i in range(len(units[g0:g0 + group]))]
        sl = {u: (slice(units[u][0] * c64, (units[u][0] + 1) * c64),
                  slice(units[u][1] * LANES, (units[u][1] + 1) * LANES)) for u in grp}
        akk, akb, ark, xs = {}, {}, {}, {}
        for u in grp:
            rs, cs = sl[u]
            a4 = _dot_nt(jnp.concatenate([stack_heads(lkk_ref[rs, cs]), stack_heads(lr_ref[rs, cs])], 0),
                         jnp.concatenate([twice(rhk_ref[rs, cs]), twice(rhb_ref[rs, cs])], 0))
            akk[u] = bf(jnp.where(strict, a4[:LANES, :LANES], 0.0))
            a_kb = jnp.where(strict, a4[:LANES, LANES:], 0.0)
            ark[u] = bf(jnp.where(incl, a4[LANES:, :LANES], 0.0))
            arb_s[u] = bf(jnp.where(incl, a4[LANES:, LANES:], 0.0))
            akb[u] = bf(a_kb)
            xs[u] = eye - jnp.where(level(1), a_kb, 0.0)
        for s in levels:
            xb = {u: bf(xs[u]) for u in grp}
            t1 = {u: bf(_dot(xb[u], akb[u] * level_b[s])) for u in grp}
            xs = {u: xs[u] - _dot(t1[u], xb[u]) for u in grp}
        xb = {u: bf(xs[u]) for u in grp}
        vv = {u: twice(v_ref[sl[u]]) for u in grp}
        for u in grp:
            xlk_s[u] = bf(_dot(xb[u], twice(lkk_ref[sl[u]])))
        av = {u: bf(_dot(akk[u], vv[u])) for u in grp}
        for u in grp:
            xavt_s[u] = _dot(xb[u], av[u]).T
        for u in grp:
            yv_s[u] = unstack(_dot(ark[u], vv[u]))
        for u in grp:
            uv_s[u] = jnp.where(same_head, _dot_tn(v_ref[sl[u]], uk_ref[sl[u]]), 0.0)

    pairs = range(A_PAIRS)
    cols = [slice(p * LANES, (p + 1) * LANES) for p in pairs]
    for c in range(nck):
        rs = slice(c * c64, (c + 1) * c64)
        us = [c * A_PAIRS + p for p in pairs]
        st = [st_ref[0, p] for p in pairs]
        stb = [bf(s_) for s_ in st]
        skt = [bf(jnp.where(same_head, _dot_nt(stb[p], xlk_s[us[p]]) + xavt_s[us[p]], 0.0))
               for p in pairs]
        upd = [_dot(skt[p], stack_heads(ub_ref[rs, cols[p]])) for p in pairs]
        for p in pairs:
            st_ref[0, p] = st[p] * wc_ref[c, :, cols[p]] + uv_s[us[p]] + upd[p]
        g_r = [_dot_nt(lr_ref[rs, cols[p]], stb[p]) for p in pairs]
        y2 = [_dot_nt(arb_s[us[p]], skt[p]) for p in pairs]
        y = jnp.concatenate([g_r[p] + yv_s[us[p]] - (y2[p][:c64] + y2[p][c64:]) for p in pairs], -1)
        mu = _half_sums(y) * (1.0 / A_HEAD)
        d = y - mu
        var = _half_sums(d * d) * (1.0 / A_HEAD)
        yn = d * lax.rsqrt(var + GN_EPS_A) * gng_ref[...] + gnb_ref[...]
        ya_ref[rs, :] = ((yn + bonus_ref[rs, :]) * g_ref[rs, :]).astype(BF16)


def _scan_multi_kernel(lkk_ref, lr_ref, rhk_ref, rhb_ref, uk_ref, ub_ref, v_ref, wc_ref, s0_ref,
                       g_ref, bonus_ref, gng_ref, gnb_ref, ya_ref, sout_ref, st_ref, *, nck, nseg):
    c64 = WKV_CHUNK
    seg = c64 // nseg
    zero = jnp.zeros((A_HEAD, A_HEAD), F32)
    for n in range(nck * nseg):
        for p in range(A_PAIRS):
            st_ref[n, p] = jnp.concatenate([jnp.concatenate([s0_ref[n, 2 * p], zero], 1),
                                            jnp.concatenate([zero, s0_ref[n, 2 * p + 1]], 1)], 0)

    ti = lax.broadcasted_iota(jnp.int32, (LANES, LANES), 0)
    tj = lax.broadcasted_iota(jnp.int32, (LANES, LANES), 1)
    same_head = (ti // c64) == (tj // c64)
    same_seg = (ti // seg) == (tj // seg)
    strict = same_seg & (tj < ti)
    incl = same_seg & (tj <= ti)
    eye = jnp.where(ti == tj, 1.0, 0.0)
    lane_row = lax.broadcasted_iota(jnp.int32, (1, LANES), 1)
    m_lo = jnp.where(lane_row < A_HEAD, 1.0, 0.0).astype(BF16)
    m_hi = jnp.where(lane_row < A_HEAD, 0.0, 1.0).astype(BF16)
    lo_half = lax.broadcasted_iota(jnp.int32, (c64, LANES), 1) < A_HEAD
    stack_heads = lambda x: jnp.concatenate([x * m_lo, x * m_hi], 0)
    twice = lambda x: jnp.concatenate([x, x], 0)
    unstack = lambda x: jnp.where(lo_half, x[:c64], x[c64:])

    bf = lambda x: x.astype(BF16)
    level = lambda s: ((ti // s) % 2 == 1) & ((tj // s) == (ti // s) - 1)
    segs = [slice(s_ * seg, (s_ + 1) * seg) for s_ in range(nseg)]

    units = [(c, p) for c in range(nck) for p in range(A_PAIRS)]
    sl = {u: (slice(u[0] * c64, (u[0] + 1) * c64), slice(u[1] * LANES, (u[1] + 1) * LANES)) for u in units}
    akk, akb, ark, arb, xs = {}, {}, {}, {}, {}
    for u in units:
        rs, cs = sl[u]
        a4 = _dot_nt(jnp.concatenate([stack_heads(lkk_ref[rs, cs]), stack_heads(lr_ref[rs, cs])], 0),
                     jnp.concatenate([twice(rhk_ref[rs, cs]), twice(rhb_ref[rs, cs])], 0))
        akk[u] = bf(jnp.where(strict, a4[:LANES, :LANES], 0.0))
        a_kb = jnp.where(strict, a4[:LANES, LANES:], 0.0)
        ark[u] = bf(jnp.where(incl, a4[LANES:, :LANES], 0.0))
        arb[u] = bf(jnp.where(incl, a4[LANES:, LANES:], 0.0))
        akb[u] = bf(a_kb)
        xs[u] = eye - jnp.where(level(1), a_kb, 0.0)
    s = 2
    while s < seg:
        lvl_b = jnp.where(level(s), 1.0, 0.0).astype(BF16)
        xb = {u: bf(xs[u]) for u in units}
        t1 = {u: bf(_dot(xb[u], akb[u] * lvl_b)) for u in units}
        xs = {u: xs[u] - _dot(t1[u], xb[u]) for u in units}
        s *= 2
    xb = {u: bf(xs[u]) for u in units}
    vv = {u: twice(v_ref[sl[u]]) for u in units}

    lkf = {u: lkk_ref[sl[u]].astype(F32) for u in units}
    lrf = {u: lr_ref[sl[u]].astype(F32) for u in units}
    g_k = {(c, p): jnp.concatenate([_dot_nt(lkf[c, p][ss], st_ref[c * nseg + i, p])
                                    for i, ss in enumerate(segs)], 0) for c, p in units}
    g_r = {(c, p): jnp.concatenate([_dot_nt(lrf[c, p][ss], st_ref[c * nseg + i, p])
                                    for i, ss in enumerate(segs)], 0) for c, p in units}
    rhs = {u: bf(twice(g_k[u]) + _dot(akk[u], vv[u])) for u in units}
    sk = {u: unstack(_dot(xb[u], rhs[u])) for u in units}
    y2 = {u: _dot(ark[u], vv[u]) - _dot(arb[u], twice(bf(sk[u]))) for u in units}
    ys = {u: g_r[u] + unstack(y2[u]) for u in units}
    for c, p in units:
        rs, cs = sl[c, p]
        vf, ukf, ubf = v_ref[rs, cs].astype(F32), uk_ref[rs, cs].astype(F32), ub_ref[rs, cs].astype(F32)
        for i, ss in enumerate(segs):
            upd = _dot_tn(jnp.concatenate([vf[ss], sk[c, p][ss]], 0),
                          jnp.concatenate([ukf[ss], ubf[ss]], 0))
            n = c * nseg + i
            new = st_ref[n, p] * wc_ref[n, :, cs] + upd
            sout_ref[n, 2 * p] = new[:A_HEAD, :A_HEAD]
            sout_ref[n, 2 * p + 1] = new[A_HEAD:, A_HEAD:]

    for c in range(nck):
        rs = slice(c * c64, (c + 1) * c64)
        y = jnp.concatenate([ys[c, p] for p in range(A_PAIRS)], -1)
        mu = _half_sums(y) * (1.0 / A_HEAD)
        d = y - mu
        var = _half_sums(d * d) * (1.0 / A_HEAD)
        yn = d * lax.rsqrt(var + GN_EPS_A) * gng_ref[...] + gnb_ref[...]
        ya_ref[rs, :] = ((yn + bonus_ref[rs, :]) * g_ref[rs, :]).astype(BF16)


def _scan(pre, s0, layer, gn_g, gn_b, nb, nj, nck, nseg, carry):
    lkk, lr, rhk, rhb, uk, ub, v, wc, g, bonus = pre
    t = lkk.shape[0]
    rows = nck * WKV_CHUNK
    blk = pl.BlockSpec((rows, A_WIDTH), lambda i, j: (i * nj + j, 0))
    wcs = pl.BlockSpec((nck * nseg, 1, A_WIDTH), lambda i, j: (i * nj + j, 0, 0))
    vec = pl.BlockSpec((1, A_WIDTH), lambda i, j: (0, 0))
    if carry:
        units = nck * A_PAIRS
        body = functools.partial(_scan_seq_kernel, nck=nck)
        s_in = s_out = pl.BlockSpec((1, A_PAIRS, LANES, LANES), lambda i, j: (i, 0, 0, 0))
        s_shape = s0.shape
        scratch = [pltpu.VMEM((units, LANES, LANES), BF16), pltpu.VMEM((units, LANES, LANES), F32),
                   pltpu.VMEM((units, LANES, LANES), BF16), pltpu.VMEM((units, WKV_CHUNK, LANES), F32),
                   pltpu.VMEM((units, LANES, LANES), F32)]
    else:
        nst = nck * nseg
        body = functools.partial(_scan_multi_kernel, nck=nck, nseg=nseg)
        s_in = pl.BlockSpec((None, nst, A_HEADS, A_HEAD, A_HEAD), lambda i, j: (layer, i, 0, 0, 0))
        s_out = pl.BlockSpec((nst, A_HEADS, A_HEAD, A_HEAD), lambda i, j: (i, 0, 0, 0))
        s_shape = s0.shape[1:]
        scratch = [pltpu.VMEM((nst, A_PAIRS, LANES, LANES), F32)]
    return pl.pallas_call(
        body,
        grid=(nb, nj),
        in_specs=[blk] * 7 + [wcs, s_in, blk, blk, vec, vec],
        out_specs=[blk, s_out],
        scratch_shapes=scratch,
        out_shape=[jax.ShapeDtypeStruct((t, A_WIDTH), BF16), jax.ShapeDtypeStruct(s_shape, F32)],
        compiler_params=pltpu.CompilerParams(dimension_semantics=("parallel", "arbitrary"),
                                             vmem_limit_bytes=VMEM_LIMIT),
        name="rwkv_scan",
    )(lkk, lr, rhk, rhb, uk, ub, v, wc, s0, g, bonus, gn_g, gn_b)


def _ret_kernel(q_ref, k_ref, v_ref, gate_ref, cos_ref, sin_ref, din_ref, dq_ref, dk_ref, dc_ref,
                s0_ref, gng_ref, gnb_ref, yb_ref, st_ref, *, nck, nseg, carry):
    ch = RET_CHUNK
    seg = ch // nseg
    if carry:
        @pl.when(pl.program_id(1) == 0)
        def _():
            st_ref[...] = s0_ref[...]
    else:
        st_ref[...] = s0_ref[...]

    lane = lax.broadcasted_iota(jnp.int32, (ch, B_QK), 1)
    first = (lane % B_DK) < (B_DK // 2)
    lane_row = lax.broadcasted_iota(jnp.int32, (1, LANES), 1)
    masks = [jnp.where(lane_row < B_DK, 1.0, 0.0), jnp.where(lane_row < B_DK, 0.0, 1.0)]

    def rotary(x, cos, sin):
        swapped = jnp.where(first, pltpu.roll(x, B_QK - B_DK // 2, axis=1),
                            pltpu.roll(x, B_DK // 2, axis=1))
        return x * cos + swapped * sin

    heads = range(B_HEADS)
    pcol = [slice((h // 2) * LANES, (h // 2 + 1) * LANES) for h in heads]
    hcol = [slice(h * B_DV, (h + 1) * B_DV) for h in heads]
    own = [slice((h % 2) * B_DK, (h % 2 + 1) * B_DK) for h in heads]
    dc = [dc_ref[h][0:1, :] for h in heads]
    zero = jnp.zeros((B_DK, B_DV), F32)
    pad = lambda h, s_: jnp.concatenate([s_, zero] if h % 2 == 0 else [zero, s_], 0)
    segs = [slice(s_ * seg, (s_ + 1) * seg) for s_ in range(nseg)]
    bf = lambda x: x.astype(BF16)
    if nseg == 1:
        st = [st_ref[0, h] for h in heads]

    for c in range(nck):
        rs = slice(c * ch, (c + 1) * ch)
        cos = jnp.concatenate([cos_ref[rs, :]] * (B_QK // LANES), -1)
        sin = jnp.concatenate([sin_ref[rs, :]] * (B_QK // LANES), -1)
        q = rotary(q_ref[rs, :].astype(F32), cos, sin)
        k = rotary(k_ref[rs, :].astype(F32), cos, sin) * (B_DK ** -0.5)
        qp = [q[:, pcol[h]] for h in heads]
        km = [k[:, pcol[h]] * masks[h % 2] for h in heads]
        vb = [v_ref[rs, hcol[h]] for h in heads]
        scores = [bf(_dot_nt(bf(qp[h]), bf(km[h])) * din_ref[h]) for h in heads]
        o = [_dot(scores[h], vb[h]) for h in heads]
        qd = [qp[h] * dq_ref[h] for h in heads]
        kd = [km[h] * dk_ref[h] for h in heads]
        if nseg == 1:
            o = [o[h] + _dot(bf(qd[h]), bf(pad(h, st[h]))) for h in heads]
            st = [st[h] * dc[h] + _dot_tn(bf(kd[h]), vb[h])[own[h]] for h in heads]
        else:
            o = [o[h] + jnp.concatenate([_dot(qd[h][ss], pad(h, st_ref[c * nseg + i, h]))
                                         for i, ss in enumerate(segs)], 0) for h in heads]
            for h in heads:
                vf = vb[h].astype(F32)
                for i, ss in enumerate(segs):
                    n = c * nseg + i
                    st_ref[n, h] = st_ref[n, h] * dc[h] + _dot_tn(kd[h][ss], vf[ss])[own[h]]
        mu = [jnp.mean(o[h], -1, keepdims=True) for h in heads]
        d = [o[h] - mu[h] for h in heads]
        var = [jnp.mean(d[h] * d[h], -1, keepdims=True) for h in heads]
        for h in heads:
            yn = d[h] * lax.rsqrt(var[h] + GN_EPS_B) * gng_ref[:, hcol[h]] + gnb_ref[:, hcol[h]]
            gate = gate_ref[rs, hcol[h]].astype(F32)
            yb_ref[rs, hcol[h]] = (yn * (gate * jax.nn.sigmoid(gate))).astype(BF16)
    if nseg == 1:
        for h in heads:
            st_ref[0, h] = st[h]


def _retention(zb, tabs, s0, layer, gn_g, gn_b, nb, nj, nck, nseg, carry, tab_rows_fixed):
    cos, sin, din, dq, dk, dc = tabs
    t = zb.shape[0]
    rows = nck * RET_CHUNK
    nst = s0.shape[1] // nb
    qs = pl.BlockSpec((rows, B_QK), lambda i, j: (i * nj + j, 0))
    ks = pl.BlockSpec((rows, B_QK), lambda i, j: (i * nj + j, 1))
    vs = pl.BlockSpec((rows, B_V), lambda i, j: (i * nj + j, 1))
    gs = pl.BlockSpec((rows, B_V), lambda i, j: (i * nj + j, 2))
    if tab_rows_fixed:
        tab = pl.BlockSpec((rows, LANES), lambda i, j: (0, 0))
    else:
        tab = pl.BlockSpec((rows, LANES), lambda i, j: (j, 0))
    s_in = pl.BlockSpec((None, nst, B_HEADS, B_DK, B_DV), lambda i, j: (layer, i, 0, 0, 0))
    sts = pl.BlockSpec((nst, B_HEADS, B_DK, B_DV), lambda i, j: (i, 0, 0, 0))
    const = lambda a: pl.BlockSpec(a.shape, lambda i, j: (0,) * a.ndim)
    return pl.pallas_call(
        functools.partial(_ret_kernel, nck=nck, nseg=nseg, carry=carry),
        grid=(nb, nj),
        in_specs=[qs, ks, vs, gs, tab, tab, const(din), const(dq), const(dk), const(dc), s_in,
                  const(gn_g), const(gn_b)],
        out_specs=[pl.BlockSpec((rows, B_V), lambda i, j: (i * nj + j, 0)), sts],
        out_shape=[jax.ShapeDtypeStruct((t, B_V), BF16), jax.ShapeDtypeStruct(s0.shape[1:], F32)],
        compiler_params=pltpu.CompilerParams(dimension_semantics=("parallel", "arbitrary"),
                                             vmem_limit_bytes=VMEM_LIMIT),
        name="retention",
    )(zb, zb, zb, zb, cos, sin, din, dq, dk, dc, s0, gn_g, gn_b)


def _merge_ffn_kernel(x_ref, ya_ref, yb_ref, zg_ref, woa_ref, wob_ref, wo_ref, gpost_ref, gpre_ref,
                      wgate_ref, wup_ref, wdown_ref, gfpost_ref, out_ref, *, nsub):
    rows = _sub_rows(x_ref.shape[0], nsub)
    sig = jax.nn.sigmoid
    pa = [_dot(ya_ref[r, :], woa_ref[...]) for r in rows]
    pb = [_dot(yb_ref[r, :], wob_ref[...]) for r in rows]
    m = [(sig(zg_ref[r, :D_MODEL].astype(F32)) * a + sig(zg_ref[r, D_MODEL:].astype(F32)) * b).astype(BF16)
         for r, a, b in zip(rows, pa, pb)]
    mo = [_dot(m_, wo_ref[...]) for m_ in m]
    x1 = [x_ref[r, :] + _rms(o, gpost_ref[...]) for r, o in zip(rows, mo)]
    hb = [_rms(x_, gpre_ref[...]).astype(BF16) for x_ in x1]
    gate = [_dot(h, wgate_ref[...]) for h in hb]
    up = [_dot(h, wup_ref[...]) for h in hb]
    act = [(g_ * sig(g_) * u).astype(BF16) for g_, u in zip(gate, up)]
    f = [_dot(a, wdown_ref[...]) for a in act]
    for r, x_, f_ in zip(rows, x1, f):
        out_ref[r, :] = x_ + _rms(f_, gfpost_ref[...])


def _merge_ffn(x, ya, yb, zg, mp, layer, tm, nsub):
    t = x.shape[0]
    row = lambda n: pl.BlockSpec((tm, n), lambda i: (i, 0))
    ws = [mp['w_out_a'], mp['w_out_b'], mp['w_o'], mp['g_post'], mp['g_ffn_pre'], mp['w_gate'],
          mp['w_up'], mp['w_down'], mp['g_ffn_post']]
    spec = lambda w: _resident_layer(w, layer) if w.ndim == 3 else _resident(w.shape)
    return pl.pallas_call(
        functools.partial(_merge_ffn_kernel, nsub=nsub),
        grid=(t // tm,),
        in_specs=[row(D_MODEL), row(A_WIDTH), row(B_V), row(2 * D_MODEL)] +
                 [spec(w) for w in ws],
        out_specs=row(D_MODEL),
        out_shape=jax.ShapeDtypeStruct((t, D_MODEL), F32),
        compiler_params=pltpu.CompilerParams(dimension_semantics=("parallel",),
                                             vmem_limit_bytes=VMEM_LIMIT),
        name="merge_ffn",
    )(x, ya, yb, zg, *ws)


def _stack_w_in(w_in):
    depth = w_in.shape[0]
    return jnp.concatenate([w_in[:, :, :SHIFT_W].astype(BF16),
                            jnp.zeros((depth, D_MODEL, SHIFT_PAD - SHIFT_W), BF16),
                            w_in[:, :, SHIFT_W:].astype(BF16)], 2)


def _layer_params(l, mu_shift, w0, lora_w_up, a0, lora_a_up, lora_g_up, k_k, k_a, r_k):
    row = lambda a: a.reshape(1, -1)
    wlora = jnp.zeros((LANES, 2 * A_WIDTH), F32)
    wlora = wlora.at[:LORA_W, :A_WIDTH].set(lora_w_up[l]).at[LORA_W:, A_WIDTH:].set(lora_a_up[l])
    lg = jnp.zeros((SHIFT_PAD - 3 * A_WIDTH - LANES, A_WIDTH), F32).at[:LORA_G].set(lora_g_up[l])
    return {
        'mu': jnp.pad(row(mu_shift[l]), ((0, 0), (0, SHIFT_PAD - SHIFT_W))),
        'wlora': wlora.astype(BF16), 'lg': lg.astype(BF16),
        'w0': row(w0[l]), 'a0': row(a0[l]), 'k_k': row(k_k[l]), 'k_a': row(k_a[l]), 'r_k': row(r_k[l]),
    }


def _ret_tables(pos, seg, rows):
    half = B_DK // 2
    inv = ROPE_BASE ** (-jnp.arange(half, dtype=F32) / half)
    ang = pos.astype(F32)[:, None] * inv[None, :]
    cos, sin = jnp.cos(ang), jnp.sin(ang)
    cos = jnp.tile(jnp.concatenate([cos, cos], -1), (1, LANES // B_DK))
    sin = jnp.tile(jnp.concatenate([-sin, sin], -1), (1, LANES // B_DK))
    log_g = jnp.log(1.0 - jnp.exp2(-5.0 - jnp.arange(B_HEADS, dtype=F32)))
    idx = jnp.arange(rows)
    loc = (idx % seg).astype(F32)
    rel = loc[:, None] - loc[None, :]
    ok = ((idx[:, None] // seg) == (idx[None, :] // seg)) & (rel >= 0)
    din = jnp.where(ok[None], jnp.exp(log_g[:, None, None] * jnp.where(ok, rel, 0.0)[None]), 0.0)
    bc = lambda col: jnp.broadcast_to(col[:, :, None], (B_HEADS, rows, LANES))
    dq = bc(jnp.exp(log_g[:, None] * (loc[None, :] + 1.0)))
    dk = bc(jnp.exp(log_g[:, None] * (seg - 1.0 - loc[None, :])))
    dc = jnp.broadcast_to(jnp.exp(log_g * seg)[:, None, None], (B_HEADS, 8, LANES))
    return cos, sin, din, dq, dk, dc


def _extract_wkv(s):
    b = s.shape[0]
    return jnp.stack([s[:, :, :A_HEAD, :A_HEAD], s[:, :, A_HEAD:, A_HEAD:]], 2).reshape(
        b, A_HEADS, A_HEAD, A_HEAD)


def _group_layer(x, nseq, seq_len, pos0, states, layer, lp, mp, gn):
    t = x.shape[0]
    long_seq = seq_len >= RET_CHUNK
    tm, nsub = 512, 2
    za, zb, zg = _in_proj(x, lp['g_pre'], lp['w_in'], layer, tm, nsub)

    if long_seq:
        wkv_nseg, ret_nseg = 1, 1
        nck_a, nck_b = 4, 2
        prep_rows = 512
        nb, nj_a, nj_b = nseq, seq_len // (nck_a * WKV_CHUNK), seq_len // (nck_b * RET_CHUNK)
        pos = pos0 + jnp.arange(seq_len)
        tabs = _ret_tables(pos, RET_CHUNK, RET_CHUNK)
    else:
        wkv_nseg, ret_nseg = WKV_CHUNK // seq_len, RET_CHUNK // seq_len
        nck_a, nck_b = 2, 1
        prep_rows = RET_CHUNK
        nb, nj_a, nj_b = t // (nck_a * WKV_CHUNK), 1, 1
        pos = pos0 + (jnp.arange(RET_CHUNK) % seq_len)
        tabs = _ret_tables(pos, seq_len, RET_CHUNK)
    nb_b = t // (nck_b * RET_CHUNK * nj_b)

    if states is None:
        s_wkv = jnp.zeros((nseq, A_PAIRS, LANES, LANES), F32)
        s_ret = jnp.zeros((1, nseq, B_HEADS, B_DK, B_DV), F32)
        shift_rows = None
        state_layer = 0
    else:
        s_wkv, s_ret = states[0], states[1]
        shift_rows = jnp.repeat(jnp.pad(states[2], ((0, 0), (0, SHIFT_PAD - SHIFT_W))), seq_len, axis=0)
        state_layer = layer

    pre = _prep(za, shift_rows, lp, seq_len, min(seq_len, WKV_CHUNK), prep_rows)
    ya, wkv = _scan(pre, s_wkv, state_layer, gn['a_g'], gn['a_b'], nb, nj_a, nck_a, wkv_nseg, long_seq)
    yb, ret = _retention(zb, tabs, s_ret, state_layer, gn['b_g'], gn['b_b'], nb_b, nj_b, nck_b, ret_nseg,
                         long_seq, not long_seq)
    x = _merge_ffn(x, ya, yb, zg, mp, layer, tm, nsub)
    shift = za.reshape(nseq, seq_len, SHIFT_PAD)[:, -1, :SHIFT_W]
    return x, (_extract_wkv(wkv) if long_seq else wkv), ret, shift


def kernel(x_prompt, x_sample, state_wkv, state_ret, state_shift, norm_mix_pre, w_in, mu_shift, w0,
           lora_w_up, a0, lora_a_up, lora_g_up, k_k, k_a, r_k, gn_a_gain, gn_a_bias, w_out_a,
           gn_b_gain, gn_b_bias, w_out_b, w_o, norm_mix_post, norm_ffn_pre, w_ffn_gate, w_ffn_up,
           w_ffn_down, norm_ffn_post):
    bp, lp_, _ = x_prompt.shape
    bs, ls_, _ = x_sample.shape
    depth = w_in.shape[0]
    yp = x_prompt.reshape(bp * lp_, D_MODEL)
    ys = x_sample.reshape(bs * ls_, D_MODEL)
    row = lambda a: a.reshape(1, -1)
    outs = [[] for _ in range(6)]
    w_in_b = _stack_w_in(w_in)
    stacks = {'w_out_a': w_out_a.astype(BF16), 'w_out_b': w_out_b.astype(BF16), 'w_o': w_o.astype(BF16),
              'w_gate': w_ffn_gate.astype(BF16), 'w_up': w_ffn_up.astype(BF16),
              'w_down': w_ffn_down.astype(BF16)}
    for l in range(depth):
        lp = _layer_params(l, mu_shift, w0, lora_w_up, a0, lora_a_up, lora_g_up, k_k, k_a, r_k)
        lp['g_pre'] = row(norm_mix_pre[l])
        lp['w_in'] = w_in_b
        mp = dict(stacks, g_post=row(norm_mix_post[l]), g_ffn_pre=row(norm_ffn_pre[l]),
                  g_ffn_post=row(norm_ffn_post[l]))
        gn = {'a_g': row(gn_a_gain[l]), 'a_b': row(gn_a_bias[l]),
              'b_g': row(gn_b_gain[l]), 'b_b': row(gn_b_bias[l])}
        yp, a_, b_, c_ = _group_layer(yp, bp, lp_, 0, None, l, lp, mp, gn)
        outs[0].append(a_); outs[1].append(b_); outs[2].append(c_)
        ys, a_, b_, c_ = _group_layer(ys, bs, ls_, PAST_LEN,
                                      (state_wkv, state_ret, state_shift[l]), l, lp, mp, gn)
        outs[3].append(a_); outs[4].append(b_); outs[5].append(c_)
    return (yp.reshape(bp, lp_, D_MODEL), ys.reshape(bs, ls_, D_MODEL)) + \
        tuple(jnp.stack(o) for o in outs)
```

```python
import functools

import jax
import jax.numpy as jnp
from jax import lax
from jax.experimental import pallas as pl
from jax.experimental.pallas import tpu as pltpu

F32 = jnp.float32
BF16 = jnp.bfloat16

LANES = 128
D_MODEL = 1024
PAST_LEN = 16384
A_HEADS = 8
A_HEAD = 64
A_WIDTH = A_HEADS * A_HEAD
A_PAIRS = A_WIDTH // LANES
LORA_W = 64
LORA_A = 64
LORA_G = 160
SHIFT_W = 3 * A_WIDTH + LORA_W + LORA_A + LORA_G
SHIFT_PAD = 1920
GN_EPS_A = 64e-5
B_HEADS = 8
B_DK = 64
B_DV = 128
B_QK = B_HEADS * B_DK
B_V = B_HEADS * B_DV
RET_W = 2 * B_QK + 2 * B_V
RET_CHUNK = 128
ROPE_BASE = 10000.0
GN_EPS_B = 1e-5
D_FF = 2816
RMS_EPS = 1e-6
WKV_CHUNK = 64

VMEM_LIMIT = 56 * 1024 * 1024


def _resident(shape):
    nd = len(shape)
    return pl.BlockSpec(shape, lambda *_: (0,) * nd, pipeline_mode=pl.Buffered(1))


def _resident_layer(a, layer):
    nd = a.ndim
    return pl.BlockSpec((None,) + a.shape[1:], lambda *_: (layer,) + (0,) * (nd - 1),
                        pipeline_mode=pl.Buffered(1))


def _dot(a, b):
    return jnp.dot(a, b, preferred_element_type=F32)


def _dot_nt(a, b):
    return lax.dot_general(a, b, (((1,), (1,)), ((), ())), preferred_element_type=F32)


def _dot_tn(a, b):
    return lax.dot_general(a, b, (((0,), (0,)), ((), ())), preferred_element_type=F32)


def _rms(x, g):
    return x * lax.rsqrt(jnp.mean(x * x, -1, keepdims=True) + RMS_EPS) * g


def _half_sums(x):
    rows, width = x.shape
    lo = lax.broadcasted_iota(jnp.int32, (rows, LANES), 1) < A_HEAD
    out = []
    for p in range(width // LANES):
        xp = x[:, p * LANES:(p + 1) * LANES]
        s_lo = jnp.sum(jnp.where(lo, xp, 0.0), -1, keepdims=True)
        s_hi = jnp.sum(jnp.where(lo, 0.0, xp), -1, keepdims=True)
        out.append(jnp.where(lo, s_lo, s_hi))
    return jnp.concatenate(out, -1)


def _sub_rows(n, nsub):
    sub = n // nsub
    return [slice(i * sub, (i + 1) * sub) for i in range(nsub)]


def _in_proj_kernel(x_ref, g_ref, w_ref, za_ref, zb_ref, zg_ref, *, nsub):
    rows = _sub_rows(x_ref.shape[0], nsub)
    hb = [_rms(x_ref[r, :], g_ref[...]).astype(BF16) for r in rows]
    col0 = 0
    for out_ref in (za_ref, zb_ref, zg_ref):
        cols = slice(col0, col0 + out_ref.shape[1])
        col0 = cols.stop
        for r, h in zip(rows, hb):
            out_ref[r, :] = _dot(h, w_ref[:, cols]).astype(out_ref.dtype)


def _in_proj(x, g, w, layer, tm, nsub):
    t = x.shape[0]
    row = lambda n: pl.BlockSpec((tm, n), lambda i: (i, 0))
    return pl.pallas_call(
        functools.partial(_in_proj_kernel, nsub=nsub),
        grid=(t // tm,),
        in_specs=[row(D_MODEL), _resident(g.shape), _resident_layer(w, layer)],
        out_specs=[row(SHIFT_PAD), row(RET_W), row(2 * D_MODEL)],
        out_shape=[jax.ShapeDtypeStruct((t, SHIFT_PAD), F32), jax.ShapeDtypeStruct((t, RET_W), BF16),
                   jax.ShapeDtypeStruct((t, 2 * D_MODEL), BF16)],
        compiler_params=pltpu.CompilerParams(dimension_semantics=("parallel",),
                                             vmem_limit_bytes=VMEM_LIMIT),
        name="in_proj",
    )(x, g, w)


def _softplus(y):
    return jnp.maximum(y, 0.0) + jnp.log(1.0 + jnp.exp(-jnp.abs(y)))


def _prep_kernel(*refs, seq_len, seg, has_shift):
    if has_shift:
        za_ref, halo_ref, shift_ref = refs[:3]
        refs = refs[3:]
    else:
        za_ref, halo_ref = refs[:2]
        shift_ref = None
        refs = refs[2:]
    (mu_ref, wlora_ref, lg_ref, w0_ref, a0_ref, kk_ref, ka_ref, rk_ref,
     lkk_ref, lr_ref, rhk_ref, rhb_ref, uk_ref, ub_ref, v_ref, wc_ref, g_ref, bonus_ref) = refs

    za = za_ref[...]
    rows = za.shape[0]
    row = lax.broadcasted_iota(jnp.int32, (rows, 1), 0)
    prev = pltpu.roll(za, 1, axis=0)
    prev = jnp.where(row == 0, halo_ref[7:8, :], prev)
    if has_shift:
        prev = jnp.where(row % seq_len == 0, shift_ref[...], prev)
    else:
        starts = (pl.program_id(0) * rows) % seq_len == 0
        prev = jnp.where((row == 0) & starts, 0.0, prev)
    zs = za + mu_ref[...] * (prev - za)

    r = zs[:, 0:A_WIDTH]
    k = zs[:, A_WIDTH:2 * A_WIDTH]
    v = zs[:, 2 * A_WIDTH:3 * A_WIDTH]
    dwa = zs[:, 3 * A_WIDTH:3 * A_WIDTH + LANES]
    dg = zs[:, 3 * A_WIDTH + LANES:SHIFT_PAD]
    lane = lax.broadcasted_iota(jnp.int32, (rows, LANES), 1)
    act = jnp.where(lane < LORA_W, jnp.tanh(dwa), dwa).astype(BF16)
    lora = _dot(act, wlora_ref[...])
    w = -_softplus(-(w0_ref[...] + lora[:, :A_WIDTH])) - 0.5
    lw = -jnp.exp(w)
    a = jax.nn.sigmoid(a0_ref[...] + lora[:, A_WIDTH:])
    g_ref[...] = _dot(jax.nn.sigmoid(dg).astype(BF16), lg_ref[...])

    kk = k * kk_ref[...]
    kk = kk * lax.rsqrt(jnp.maximum(_half_sums(kk * kk), 1e-24))
    k2 = k * (1.0 + (a - 1.0) * ka_ref[...])
    b = kk * a
    bonus_ref[...] = _half_sums(r * k2 * rk_ref[...]) * v
    v_ref[...] = v.astype(BF16)

    ti = lax.broadcasted_iota(jnp.int32, (LANES, LANES), 0)
    tj = lax.broadcasted_iota(jnp.int32, (LANES, LANES), 1)
    same = (ti // seg) == (tj // seg)
    tri = jnp.concatenate([jnp.where(same & (tj <= ti), 1.0, 0.0),
                           jnp.where(same, 1.0, 0.0)], 0).astype(BF16)
    for g0 in range(0, rows, LANES):
        sl = slice(g0, g0 + LANES)
        x = lw[sl]
        hi = x.astype(BF16)
        r1 = x - hi.astype(F32)
        mid = r1.astype(BF16)
        lo = (r1 - mid.astype(F32)).astype(BF16)
        ct = _dot(tri, hi) + _dot(tri, mid) + _dot(tri, lo)
        cum, tot = ct[:LANES], ct[LANES:]
        inv = jnp.exp(-cum)
        tail = jnp.exp(tot - cum)
        lkk_ref[sl, :] = (kk[sl] * jnp.exp(cum - x)).astype(BF16)
        lr_ref[sl, :] = (r[sl] * jnp.exp(cum)).astype(BF16)
        rhk_ref[sl, :] = (k2[sl] * inv).astype(BF16)
        rhb_ref[sl, :] = (b[sl] * inv).astype(BF16)
        uk_ref[sl, :] = (k2[sl] * tail).astype(BF16)
        ub_ref[sl, :] = (-(b[sl] * tail)).astype(BF16)
        for s in range(LANES // seg):
            wc_ref[g0 // seg + s] = jnp.exp(tot[s * seg:s * seg + 1, :])


def _prep(za, shift_rows, pp, seq_len, seg, rows):
    t = za.shape[0]
    has_shift = shift_rows is not None
    blk = lambda n: pl.BlockSpec((rows, n), lambda i: (i, 0))
    halo = pl.BlockSpec((8, SHIFT_PAD), lambda i: (jnp.maximum(i * (rows // 8) - 1, 0), 0))
    params = [pp['mu'], pp['wlora'], pp['lg'], pp['w0'], pp['a0'], pp['k_k'], pp['k_a'], pp['r_k']]
    ins = [za, za] + ([shift_rows] if has_shift else []) + params
    in_specs = [blk(SHIFT_PAD), halo] + ([blk(SHIFT_PAD)] if has_shift else []) + \
        [_resident(p.shape) for p in params]
    bf = jax.ShapeDtypeStruct((t, A_WIDTH), BF16)
    f32 = jax.ShapeDtypeStruct((t, A_WIDTH), F32)
    out_shape = [bf] * 7 + [jax.ShapeDtypeStruct((t // seg, 1, A_WIDTH), F32), f32, f32]
    out_specs = [blk(A_WIDTH)] * 7 + [pl.BlockSpec((rows // seg, 1, A_WIDTH), lambda i: (i, 0, 0)),
                                      blk(A_WIDTH), blk(A_WIDTH)]
    return pl.pallas_call(
        functools.partial(_prep_kernel, seq_len=seq_len, seg=seg, has_shift=has_shift),
        grid=(t // rows,),
        in_specs=in_specs, out_specs=out_specs, out_shape=out_shape,
        compiler_params=pltpu.CompilerParams(dimension_semantics=("parallel",),
                                             vmem_limit_bytes=VMEM_LIMIT),
        name="rwkv_prep",
    )(*ins)


def _scan_seq_kernel(lkk_ref, lr_ref, rhk_ref, rhb_ref, uk_ref, ub_ref, v_ref, wc_ref, s0_ref,
                     g_ref, bonus_ref, gng_ref, gnb_ref, ya_ref, st_ref,
                     xlk_s, xavt_s, arb_s, yv_s, uv_s, *, nck):
    c64 = WKV_CHUNK

    @pl.when(pl.program_id(1) == 0)
    def _():
        st_ref[...] = s0_ref[...]

    ti = lax.broadcasted_iota(jnp.int32, (LANES, LANES), 0)
    tj = lax.broadcasted_iota(jnp.int32, (LANES, LANES), 1)
    same_head = (ti // c64) == (tj // c64)
    strict = same_head & (tj < ti)
    incl = same_head & (tj <= ti)
    eye = jnp.where(ti == tj, 1.0, 0.0)
    level = lambda s: ((ti // s) % 2 == 1) & ((tj // s) == (ti // s) - 1)
    levels = [2 ** e for e in range(1, 6)]
    level_b = {s: jnp.where(level(s), 1.0, 0.0).astype(BF16) for s in levels}
    lane_row = lax.broadcasted_iota(jnp.int32, (1, LANES), 1)
    m_lo = jnp.where(lane_row < A_HEAD, 1.0, 0.0).astype(BF16)
    m_hi = jnp.where(lane_row < A_HEAD, 0.0, 1.0).astype(BF16)
    lo_half = lax.broadcasted_iota(jnp.int32, (c64, LANES), 1) < A_HEAD
    stack_heads = lambda x: jnp.concatenate([x * m_lo, x * m_hi], 0)
    twice = lambda x: jnp.concatenate([x, x], 0)
    unstack = lambda x: jnp.where(lo_half, x[:c64], x[c64:])
    bf = lambda x: x.astype(BF16)

    units = [(c, p) for c in range(nck) for p in range(A_PAIRS)]
    group = 16
    for g0 in range(0, len(units), group):
        grp = [g0 + i for i in range(len(units[g0:g0 + group]))]
        sl = {u: (slice(units[u][0] * c64, (units[u][0] + 1) * c64),
                  slice(units[u][1] * LANES, (units[u][1] + 1) * LANES)) for u in grp}
        akk, akb, ark, xs = {}, {}, {}, {}
        for u in grp:
            rs, cs = sl[u]
            a4 = _dot_nt(jnp.concatenate([stack_heads(lkk_ref[rs, cs]), stack_heads(lr_ref[rs, cs])], 0),
                         jnp.concatenate([twice(rhk_ref[rs, cs]), twice(rhb_ref[rs, cs])], 0))
            akk[u] = bf(jnp.where(strict, a4[:LANES, :LANES], 0.0))
            a_kb = jnp.where(strict, a4[:LANES, LANES:], 0.0)
            ark[u] = bf(jnp.where(incl, a4[LANES:, :LANES], 0.0))
            arb_s[u] = bf(jnp.where(incl, a4[LANES:, LANES:], 0.0))
            akb[u] = bf(a_kb)
            xs[u] = eye - jnp.where(level(1), a_kb, 0.0)
        for s in levels:
            xb = {u: bf(xs[u]) for u in grp}
            t1 = {u: bf(_dot(xb[u], akb[u] * level_b[s])) for u in grp}
            xs = {u: xs[u] - _dot(t1[u], xb[u]) for u in grp}
        xb = {u: bf(xs[u]) for u in grp}
        vv = {u: twice(v_ref[sl[u]]) for u in grp}
        for u in grp:
            xlk_s[u] = bf(_dot(xb[u], twice(lkk_ref[sl[u]])))
        av = {u: bf(_dot(akk[u], vv[u])) for u in grp}
        for u in grp:
            xavt_s[u] = _dot(xb[u], av[u]).T
        for u in grp:
            yv_s[u] = unstack(_dot(ark[u], vv[u]))
        for u in grp:
            uv_s[u] = jnp.where(same_head, _dot_tn(v_ref[sl[u]], uk_ref[sl[u]]), 0.0)

    pairs = range(A_PAIRS)
    cols = [slice(p * LANES, (p + 1) * LANES) for p in pairs]
    for c in range(nck):
        rs = slice(c * c64, (c + 1) * c64)
        us = [c * A_PAIRS + p for p in pairs]
        st = [st_ref[0, p] for p in pairs]
        stb = [bf(s_) for s_ in st]
        skt = [bf(jnp.where(same_head, _dot_nt(stb[p], xlk_s[us[p]]) + xavt_s[us[p]], 0.0))
               for p in pairs]
        upd = [_dot(skt[p], stack_heads(ub_ref[rs, cols[p]])) for p in pairs]
        for p in pairs:
            st_ref[0, p] = st[p] * wc_ref[c, :, cols[p]] + uv_s[us[p]] + upd[p]
        g_r = [_dot_nt(lr_ref[rs, cols[p]], stb[p]) for p in pairs]
        y2 = [_dot_nt(arb_s[us[p]], skt[p]) for p in pairs]
        y = jnp.concatenate([g_r[p] + yv_s[us[p]] - (y2[p][:c64] + y2[p][c64:]) for p in pairs], -1)
        mu = _half_sums(y) * (1.0 / A_HEAD)
        d = y - mu
        var = _half_sums(d * d) * (1.0 / A_HEAD)
        yn = d * lax.rsqrt(var + GN_EPS_A) * gng_ref[...] + gnb_ref[...]
        ya_ref[rs, :] = ((yn + bonus_ref[rs, :]) * g_ref[rs, :]).astype(BF16)


def _scan_multi_kernel(lkk_ref, lr_ref, rhk_ref, rhb_ref, uk_ref, ub_ref, v_ref, wc_ref, s0_ref,
                       g_ref, bonus_ref, gng_ref, gnb_ref, ya_ref, sout_ref, st_ref, *, nck, nseg):
    c64 = WKV_CHUNK
    seg = c64 // nseg
    zero = jnp.zeros((A_HEAD, A_HEAD), F32)
    for n in range(nck * nseg):
        for p in range(A_PAIRS):
            st_ref[n, p] = jnp.concatenate([jnp.concatenate([s0_ref[n, 2 * p], zero], 1),
                                            jnp.concatenate([zero, s0_ref[n, 2 * p + 1]], 1)], 0)

    ti = lax.broadcasted_iota(jnp.int32, (LANES, LANES), 0)
    tj = lax.broadcasted_iota(jnp.int32, (LANES, LANES), 1)
    same_head = (ti // c64) == (tj // c64)
    same_seg = (ti // seg) == (tj // seg)
    strict = same_seg & (tj < ti)
    incl = same_seg & (tj <= ti)
    eye = jnp.where(ti == tj, 1.0, 0.0)
    lane_row = lax.broadcasted_iota(jnp.int32, (1, LANES), 1)
    m_lo = jnp.where(lane_row < A_HEAD, 1.0, 0.0).astype(BF16)
    m_hi = jnp.where(lane_row < A_HEAD, 0.0, 1.0).astype(BF16)
    lo_half = lax.broadcasted_iota(jnp.int32, (c64, LANES), 1) < A_HEAD
    stack_heads = lambda x: jnp.concatenate([x * m_lo, x * m_hi], 0)
    twice = lambda x: jnp.concatenate([x, x], 0)
    unstack = lambda x: jnp.where(lo_half, x[:c64], x[c64:])

    bf = lambda x: x.astype(BF16)
    level = lambda s: ((ti // s) % 2 == 1) & ((tj // s) == (ti // s) - 1)
    segs = [slice(s_ * seg, (s_ + 1) * seg) for s_ in range(nseg)]

    units = [(c, p) for c in range(nck) for p in range(A_PAIRS)]
    sl = {u: (slice(u[0] * c64, (u[0] + 1) * c64), slice(u[1] * LANES, (u[1] + 1) * LANES)) for u in units}
    akk, akb, ark, arb, xs = {}, {}, {}, {}, {}
    for u in units:
        rs, cs = sl[u]
        a4 = _dot_nt(jnp.concatenate([stack_heads(lkk_ref[rs, cs]), stack_heads(lr_ref[rs, cs])], 0),
                     jnp.concatenate([twice(rhk_ref[rs, cs]), twice(rhb_ref[rs, cs])], 0))
        akk[u] = bf(jnp.where(strict, a4[:LANES, :LANES], 0.0))
        a_kb = jnp.where(strict, a4[:LANES, LANES:], 0.0)
        ark[u] = bf(jnp.where(incl, a4[LANES:, :LANES], 0.0))
        arb[u] = bf(jnp.where(incl, a4[LANES:, LANES:], 0.0))
        akb[u] = bf(a_kb)
        xs[u] = eye - jnp.where(level(1), a_kb, 0.0)
    s = 2
    while s < seg:
        lvl_b = jnp.where(level(s), 1.0, 0.0).astype(BF16)
        xb = {u: bf(xs[u]) for u in units}
        t1 = {u: bf(_dot(xb[u], akb[u] * lvl_b)) for u in units}
        xs = {u: xs[u] - _dot(t1[u], xb[u]) for u in units}
        s *= 2
    xb = {u: bf(xs[u]) for u in units}
    vv = {u: twice(v_ref[sl[u]]) for u in units}

    lkf = {u: lkk_ref[sl[u]].astype(F32) for u in units}
    lrf = {u: lr_ref[sl[u]].astype(F32) for u in units}
    g_k = {(c, p): jnp.concatenate([_dot_nt(lkf[c, p][ss], st_ref[c * nseg + i, p])
                                    for i, ss in enumerate(segs)], 0) for c, p in units}
    g_r = {(c, p): jnp.concatenate([_dot_nt(lrf[c, p][ss], st_ref[c * nseg + i, p])
                                    for i, ss in enumerate(segs)], 0) for c, p in units}
    rhs = {u: bf(twice(g_k[u]) + _dot(akk[u], vv[u])) for u in units}
    sk = {u: unstack(_dot(xb[u], rhs[u])) for u in units}
    y2 = {u: _dot(ark[u], vv[u]) - _dot(arb[u], twice(bf(sk[u]))) for u in units}
    ys = {u: g_r[u] + unstack(y2[u]) for u in units}
    for c, p in units:
        rs, cs = sl[c, p]
        vf, ukf, ubf = v_ref[rs, cs].astype(F32), uk_ref[rs, cs].astype(F32), ub_ref[rs, cs].astype(F32)
        for i, ss in enumerate(segs):
            upd = _dot_tn(jnp.concatenate([vf[ss], sk[c, p][ss]], 0),
                          jnp.concatenate([ukf[ss], ubf[ss]], 0))
            n = c * nseg + i
            new = st_ref[n, p] * wc_ref[n, :, cs] + upd
            sout_ref[n, 2 * p] = new[:A_HEAD, :A_HEAD]
            sout_ref[n, 2 * p + 1] = new[A_HEAD:, A_HEAD:]

    for c in range(nck):
        rs = slice(c * c64, (c + 1) * c64)
        y = jnp.concatenate([ys[c, p] for p in range(A_PAIRS)], -1)
        mu = _half_sums(y) * (1.0 / A_HEAD)
        d = y - mu
        var = _half_sums(d * d) * (1.0 / A_HEAD)
        yn = d * lax.rsqrt(var + GN_EPS_A) * gng_ref[...] + gnb_ref[...]
        ya_ref[rs, :] = ((yn + bonus_ref[rs, :]) * g_ref[rs, :]).astype(BF16)


def _scan(pre, s0, layer, gn_g, gn_b, nb, nj, nck, nseg, carry):
    lkk, lr, rhk, rhb, uk, ub, v, wc, g, bonus = pre
    t = lkk.shape[0]
    rows = nck * WKV_CHUNK
    blk = pl.BlockSpec((rows, A_WIDTH), lambda i, j: (i * nj + j, 0))
    wcs = pl.BlockSpec((nck * nseg, 1, A_WIDTH), lambda i, j: (i * nj + j, 0, 0))
    vec = pl.BlockSpec((1, A_WIDTH), lambda i, j: (0, 0))
    if carry:
        units = nck * A_PAIRS
        body = functools.partial(_scan_seq_kernel, nck=nck)
        s_in = s_out = pl.BlockSpec((1, A_PAIRS, LANES, LANES), lambda i, j: (i, 0, 0, 0))
        s_shape = s0.shape
        scratch = [pltpu.VMEM((units, LANES, LANES), BF16), pltpu.VMEM((units, LANES, LANES), F32),
                   pltpu.VMEM((units, LANES, LANES), BF16), pltpu.VMEM((units, WKV_CHUNK, LANES), F32),
                   pltpu.VMEM((units, LANES, LANES), F32)]
    else:
        nst = nck * nseg
        body = functools.partial(_scan_multi_kernel, nck=nck, nseg=nseg)
        s_in = pl.BlockSpec((None, nst, A_HEADS, A_HEAD, A_HEAD), lambda i, j: (layer, i, 0, 0, 0))
        s_out = pl.BlockSpec((nst, A_HEADS, A_HEAD, A_HEAD), lambda i, j: (i, 0, 0, 0))
        s_shape = s0.shape[1:]
        scratch = [pltpu.VMEM((nst, A_PAIRS, LANES, LANES), F32)]
    return pl.pallas_call(
        body,
        grid=(nb, nj),
        in_specs=[blk] * 7 + [wcs, s_in, blk, blk, vec, vec],
        out_specs=[blk, s_out],
        scratch_shapes=scratch,
        out_shape=[jax.ShapeDtypeStruct((t, A_WIDTH), BF16), jax.ShapeDtypeStruct(s_shape, F32)],
        compiler_params=pltpu.CompilerParams(dimension_semantics=("parallel", "arbitrary"),
                                             vmem_limit_bytes=VMEM_LIMIT),
        name="rwkv_scan",
    )(lkk, lr, rhk, rhb, uk, ub, v, wc, s0, g, bonus, gn_g, gn_b)


def _ret_kernel(q_ref, k_ref, v_ref, gate_ref, cos_ref, sin_ref, din_ref, dq_ref, dk_ref, dc_ref,
                s0_ref, gng_ref, gnb_ref, yb_ref, st_ref, *, nck, nseg, carry):
    ch = RET_CHUNK
    seg = ch // nseg
    if carry:
        @pl.when(pl.program_id(1) == 0)
        def _():
            st_ref[...] = s0_ref[...]
    else:
        st_ref[...] = s0_ref[...]

    lane = lax.broadcasted_iota(jnp.int32, (ch, B_QK), 1)
    first = (lane % B_DK) < (B_DK // 2)
    lane_row = lax.broadcasted_iota(jnp.int32, (1, LANES), 1)
    masks = [jnp.where(lane_row < B_DK, 1.0, 0.0), jnp.where(lane_row < B_DK, 0.0, 1.0)]

    def rotary(x, cos, sin):
        swapped = jnp.where(first, pltpu.roll(x, B_QK - B_DK // 2, axis=1),
                            pltpu.roll(x, B_DK // 2, axis=1))
        return x * cos + swapped * sin

    heads = range(B_HEADS)
    pcol = [slice((h // 2) * LANES, (h // 2 + 1) * LANES) for h in heads]
    hcol = [slice(h * B_DV, (h + 1) * B_DV) for h in heads]
    own = [slice((h % 2) * B_DK, (h % 2 + 1) * B_DK) for h in heads]
    dc = [dc_ref[h][0:1, :] for h in heads]
    zero = jnp.zeros((B_DK, B_DV), F32)
    pad = lambda h, s_: jnp.concatenate([s_, zero] if h % 2 == 0 else [zero, s_], 0)
    segs = [slice(s_ * seg, (s_ + 1) * seg) for s_ in range(nseg)]
    bf = lambda x: x.astype(BF16)
    if nseg == 1:
        st = [st_ref[0, h] for h in heads]

    for c in range(nck):
        rs = slice(c * ch, (c + 1) * ch)
        cos = jnp.concatenate([cos_ref[rs, :]] * (B_QK // LANES), -1)
        sin = jnp.concatenate([sin_ref[rs, :]] * (B_QK // LANES), -1)
        q = rotary(q_ref[rs, :].astype(F32), cos, sin)
        k = rotary(k_ref[rs, :].astype(F32), cos, sin) * (B_DK ** -0.5)
        qp = [q[:, pcol[h]] for h in heads]
        km = [k[:, pcol[h]] * masks[h % 2] for h in heads]
        vb = [v_ref[rs, hcol[h]] for h in heads]
        scores = [bf(_dot_nt(bf(qp[h]), bf(km[h])) * din_ref[h]) for h in heads]
        o = [_dot(scores[h], vb[h]) for h in heads]
        qd = [qp[h] * dq_ref[h] for h in heads]
        kd = [km[h] * dk_ref[h] for h in heads]
        if nseg == 1:
            o = [o[h] + _dot(bf(qd[h]), bf(pad(h, st[h]))) for h in heads]
            st = [st[h] * dc[h] + _dot_tn(bf(kd[h]), vb[h])[own[h]] for h in heads]
        else:
            o = [o[h] + jnp.concatenate([_dot(qd[h][ss], pad(h, st_ref[c * nseg + i, h]))
                                         for i, ss in enumerate(segs)], 0) for h in heads]
            for h in heads:
                vf = vb[h].astype(F32)
                for i, ss in enumerate(segs):
                    n = c * nseg + i
                    st_ref[n, h] = st_ref[n, h] * dc[h] + _dot_tn(kd[h][ss], vf[ss])[own[h]]
        mu = [jnp.mean(o[h], -1, keepdims=True) for h in heads]
        d = [o[h] - mu[h] for h in heads]
        var = [jnp.mean(d[h] * d[h], -1, keepdims=True) for h in heads]
        for h in heads:
            yn = d[h] * lax.rsqrt(var[h] + GN_EPS_B) * gng_ref[:, hcol[h]] + gnb_ref[:, hcol[h]]
            gate = gate_ref[rs, hcol[h]].astype(F32)
            yb_ref[rs, hcol[h]] = (yn * (gate * jax.nn.sigmoid(gate))).astype(BF16)
    if nseg == 1:
        for h in heads:
            st_ref[0, h] = st[h]


def _retention(zb, tabs, s0, layer, gn_g, gn_b, nb, nj, nck, nseg, carry, tab_rows_fixed):
    cos, sin, din, dq, dk, dc = tabs
    t = zb.shape[0]
    rows = nck * RET_CHUNK
    nst = s0.shape[1] // nb
    qs = pl.BlockSpec((rows, B_QK), lambda i, j: (i * nj + j, 0))
    ks = pl.BlockSpec((rows, B_QK), lambda i, j: (i * nj + j, 1))
    vs = pl.BlockSpec((rows, B_V), lambda i, j: (i * nj + j, 1))
    gs = pl.BlockSpec((rows, B_V), lambda i, j: (i * nj + j, 2))
    if tab_rows_fixed:
        tab = pl.BlockSpec((rows, LANES), lambda i, j: (0, 0))
    else:
        tab = pl.BlockSpec((rows, LANES), lambda i, j: (j, 0))
    s_in = pl.BlockSpec((None, nst, B_HEADS, B_DK, B_DV), lambda i, j: (layer, i, 0, 0, 0))
    sts = pl.BlockSpec((nst, B_HEADS, B_DK, B_DV), lambda i, j: (i, 0, 0, 0))
    const = lambda a: pl.BlockSpec(a.shape, lambda i, j: (0,) * a.ndim)
    return pl.pallas_call(
        functools.partial(_ret_kernel, nck=nck, nseg=nseg, carry=carry),
        grid=(nb, nj),
        in_specs=[qs, ks, vs, gs, tab, tab, const(din), const(dq), const(dk), const(dc), s_in,
                  const(gn_g), const(gn_b)],
        out_specs=[pl.BlockSpec((rows, B_V), lambda i, j: (i * nj + j, 0)), sts],
        out_shape=[jax.ShapeDtypeStruct((t, B_V), BF16), jax.ShapeDtypeStruct(s0.shape[1:], F32)],
        compiler_params=pltpu.CompilerParams(dimension_semantics=("parallel", "arbitrary"),
                                             vmem_limit_bytes=VMEM_LIMIT),
        name="retention",
    )(zb, zb, zb, zb, cos, sin, din, dq, dk, dc, s0, gn_g, gn_b)


def _merge_ffn_kernel(x_ref, ya_ref, yb_ref, zg_ref, woa_ref, wob_ref, wo_ref, gpost_ref, gpre_ref,
                      wgate_ref, wup_ref, wdown_ref, gfpost_ref, out_ref, *, nsub):
    rows = _sub_rows(x_ref.shape[0], nsub)
    sig = jax.nn.sigmoid
    pa = [_dot(ya_ref[r, :], woa_ref[...]) for r in rows]
    pb = [_dot(yb_ref[r, :], wob_ref[...]) for r in rows]
    m = [(sig(zg_ref[r, :D_MODEL].astype(F32)) * a + sig(zg_ref[r, D_MODEL:].astype(F32)) * b).astype(BF16)
         for r, a, b in zip(rows, pa, pb)]
    mo = [_dot(m_, wo_ref[...]) for m_ in m]
    x1 = [x_ref[r, :] + _rms(o, gpost_ref[...]) for r, o in zip(rows, mo)]
    hb = [_rms(x_, gpre_ref[...]).astype(BF16) for x_ in x1]
    gate = [_dot(h, wgate_ref[...]) for h in hb]
    up = [_dot(h, wup_ref[...]) for h in hb]
    act = [(g_ * sig(g_) * u).astype(BF16) for g_, u in zip(gate, up)]
    f = [_dot(a, wdown_ref[...]) for a in act]
    for r, x_, f_ in zip(rows, x1, f):
        out_ref[r, :] = x_ + _rms(f_, gfpost_ref[...])


def _merge_ffn(x, ya, yb, zg, mp, layer, tm, nsub):
    t = x.shape[0]
    row = lambda n: pl.BlockSpec((tm, n), lambda i: (i, 0))
    ws = [mp['w_out_a'], mp['w_out_b'], mp['w_o'], mp['g_post'], mp['g_ffn_pre'], mp['w_gate'],
          mp['w_up'], mp['w_down'], mp['g_ffn_post']]
    spec = lambda w: _resident_layer(w, layer) if w.ndim == 3 else _resident(w.shape)
    return pl.pallas_call(
        functools.partial(_merge_ffn_kernel, nsub=nsub),
        grid=(t // tm,),
        in_specs=[row(D_MODEL), row(A_WIDTH), row(B_V), row(2 * D_MODEL)] +
                 [spec(w) for w in ws],
        out_specs=row(D_MODEL),
        out_shape=jax.ShapeDtypeStruct((t, D_MODEL), F32),
        compiler_params=pltpu.CompilerParams(dimension_semantics=("parallel",),
                                             vmem_limit_bytes=VMEM_LIMIT),
        name="merge_ffn",
    )(x, ya, yb, zg, *ws)


def _stack_w_in(w_in):
    depth = w_in.shape[0]
    return jnp.concatenate([w_in[:, :, :SHIFT_W].astype(BF16),
                            jnp.zeros((depth, D_MODEL, SHIFT_PAD - SHIFT_W), BF16),
                            w_in[:, :, SHIFT_W:].astype(BF16)], 2)


def _layer_params(l, mu_shift, w0, lora_w_up, a0, lora_a_up, lora_g_up, k_k, k_a, r_k):
    row = lambda a: a.reshape(1, -1)
    wlora = jnp.zeros((LANES, 2 * A_WIDTH), F32)
    wlora = wlora.at[:LORA_W, :A_WIDTH].set(lora_w_up[l]).at[LORA_W:, A_WIDTH:].set(lora_a_up[l])
    lg = jnp.zeros((SHIFT_PAD - 3 * A_WIDTH - LANES, A_WIDTH), F32).at[:LORA_G].set(lora_g_up[l])
    return {
        'mu': jnp.pad(row(mu_shift[l]), ((0, 0), (0, SHIFT_PAD - SHIFT_W))),
        'wlora': wlora.astype(BF16), 'lg': lg.astype(BF16),
        'w0': row(w0[l]), 'a0': row(a0[l]), 'k_k': row(k_k[l]), 'k_a': row(k_a[l]), 'r_k': row(r_k[l]),
    }


def _ret_tables(pos, seg, rows):
    half = B_DK // 2
    inv = ROPE_BASE ** (-jnp.arange(half, dtype=F32) / half)
    ang = pos.astype(F32)[:, None] * inv[None, :]
    cos, sin = jnp.cos(ang), jnp.sin(ang)
    cos = jnp.tile(jnp.concatenate([cos, cos], -1), (1, LANES // B_DK))
    sin = jnp.tile(jnp.concatenate([-sin, sin], -1), (1, LANES // B_DK))
    log_g = jnp.log(1.0 - jnp.exp2(-5.0 - jnp.arange(B_HEADS, dtype=F32)))
    idx = jnp.arange(rows)
    loc = (idx % seg).astype(F32)
    rel = loc[:, None] - loc[None, :]
    ok = ((idx[:, None] // seg) == (idx[None, :] // seg)) & (rel >= 0)
    din = jnp.where(ok[None], jnp.exp(log_g[:, None, None] * jnp.where(ok, rel, 0.0)[None]), 0.0)
    bc = lambda col: jnp.broadcast_to(col[:, :, None], (B_HEADS, rows, LANES))
    dq = bc(jnp.exp(log_g[:, None] * (loc[None, :] + 1.0)))
    dk = bc(jnp.exp(log_g[:, None] * (seg - 1.0 - loc[None, :])))
    dc = jnp.broadcast_to(jnp.exp(log_g * seg)[:, None, None], (B_HEADS, 8, LANES))
    return cos, sin, din, dq, dk, dc


def _extract_wkv(s):
    b = s.shape[0]
    return jnp.stack([s[:, :, :A_HEAD, :A_HEAD], s[:, :, A_HEAD:, A_HEAD:]], 2).reshape(
        b, A_HEADS, A_HEAD, A_HEAD)


def _group_layer(x, nseq, seq_len, pos0, states, layer, lp, mp, gn):
    t = x.shape[0]
    long_seq = seq_len >= RET_CHUNK
    tm, nsub = 512, 2
    za, zb, zg = _in_proj(x, lp['g_pre'], lp['w_in'], layer, tm, nsub)

    if long_seq:
        wkv_nseg, ret_nseg = 1, 1
        nck_a, nck_b = 8, 2
        prep_rows = 512
        nb, nj_a, nj_b = nseq, seq_len // (nck_a * WKV_CHUNK), seq_len // (nck_b * RET_CHUNK)
        pos = pos0 + jnp.arange(seq_len)
        tabs = _ret_tables(pos, RET_CHUNK, RET_CHUNK)
    else:
        wkv_nseg, ret_nseg = WKV_CHUNK // seq_len, RET_CHUNK // seq_len
        nck_a, nck_b = 2, 1
        prep_rows = RET_CHUNK
        nb, nj_a, nj_b = t // (nck_a * WKV_CHUNK), 1, 1
        pos = pos0 + (jnp.arange(RET_CHUNK) % seq_len)
        tabs = _ret_tables(pos, seq_len, RET_CHUNK)
    nb_b = t // (nck_b * RET_CHUNK * nj_b)

    if states is None:
        s_wkv = jnp.zeros((nseq, A_PAIRS, LANES, LANES), F32)
        s_ret = jnp.zeros((1, nseq, B_HEADS, B_DK, B_DV), F32)
        shift_rows = None
        state_layer = 0
    else:
        s_wkv, s_ret = states[0], states[1]
        shift_rows = jnp.repeat(jnp.pad(states[2], ((0, 0), (0, SHIFT_PAD - SHIFT_W))), seq_len, axis=0)
        state_layer = layer

    pre = _prep(za, shift_rows, lp, seq_len, min(seq_len, WKV_CHUNK), prep_rows)
    ya, wkv = _scan(pre, s_wkv, state_layer, gn['a_g'], gn['a_b'], nb, nj_a, nck_a, wkv_nseg, long_seq)
    yb, ret = _retention(zb, tabs, s_ret, state_layer, gn['b_g'], gn['b_b'], nb_b, nj_b, nck_b, ret_nseg,
                         long_seq, not long_seq)
    x = _merge_ffn(x, ya, yb, zg, mp, layer, tm, nsub)
    shift = za.reshape(nseq, seq_len, SHIFT_PAD)[:, -1, :SHIFT_W]
    return x, (_extract_wkv(wkv) if long_seq else wkv), ret, shift


def kernel(x_prompt, x_sample, state_wkv, state_ret, state_shift, norm_mix_pre, w_in, mu_shift, w0,
           lora_w_up, a0, lora_a_up, lora_g_up, k_k, k_a, r_k, gn_a_gain, gn_a_bias, w_out_a,
           gn_b_gain, gn_b_bias, w_out_b, w_o, norm_mix_post, norm_ffn_pre, w_ffn_gate, w_ffn_up,
           w_ffn_down, norm_ffn_post):
    bp, lp_, _ = x_prompt.shape
    bs, ls_, _ = x_sample.shape
    depth = w_in.shape[0]
    yp = x_prompt.reshape(bp * lp_, D_MODEL)
    ys = x_sample.reshape(bs * ls_, D_MODEL)
    row = lambda a: a.reshape(1, -1)
    outs = [[] for _ in range(6)]
    w_in_b = _stack_w_in(w_in)
    stacks = {'w_out_a': w_out_a.astype(BF16), 'w_out_b': w_out_b.astype(BF16), 'w_o': w_o.astype(BF16),
              'w_gate': w_ffn_gate.astype(BF16), 'w_up': w_ffn_up.astype(BF16),
              'w_down': w_ffn_down.astype(BF16)}
    for l in range(depth):
        lp = _layer_params(l, mu_shift, w0, lora_w_up, a0, lora_a_up, lora_g_up, k_k, k_a, r_k)
        lp['g_pre'] = row(norm_mix_pre[l])
        lp['w_in'] = w_in_b
        mp = dict(stacks, g_post=row(norm_mix_post[l]), g_ffn_pre=row(norm_ffn_pre[l]),
                  g_ffn_post=row(norm_ffn_post[l]))
        gn = {'a_g': row(gn_a_gain[l]), 'a_b': row(gn_a_bias[l]),
              'b_g': row(gn_b_gain[l]), 'b_b': row(gn_b_bias[l])}
        yp, a_, b_, c_ = _group_layer(yp, bp, lp_, 0, None, l, lp, mp, gn)
        outs[0].append(a_); outs[1].append(b_); outs[2].append(c_)
        ys, a_, b_, c_ = _group_layer(ys, bs, ls_, PAST_LEN,
                                      (state_wkv, state_ret, state_shift[l]), l, lp, mp, gn)
        outs[3].append(a_); outs[4].append(b_); outs[5].append(c_)
    return (yp.reshape(bp, lp_, D_MODEL), ys.reshape(bs, ls_, D_MODEL)) + \
        tuple(jnp.stack(o) for o in outs)
```
